```python
import functools
import math
import jax
import jax.numpy as jnp
from jax import lax
import numpy as np

D_MODEL = 1024
BATCH = 4
SEQ = 4096
DEPTH = 4
DEC_BATCH = 128
DEC_SEQ = 1
PAST_LEN = 2048
PAGE_SIZE = 128

A_HEADS = 8
A_HEAD_DIM = 64
A_WIDTH = A_HEADS * A_HEAD_DIM
IDX_HEADS = 4
IDX_DIM = 64
TOPK_MAX = 256
Q_BLOCK = 128
ROPE_THETA = 10000.0
S5_GROUP_CH = 16
S5_GROUPS = 16
S5_WIDTH = S5_GROUPS * S5_GROUP_CH
S5_STATE = 64
DT_MIN = 1e-3
DT_MAX = 1e-1
SG_CHUNK = 128
SG_GROUPS = 4
SG_WIDTH = 256
SG_GROUP_CH = SG_WIDTH // SG_GROUPS
MEM_TOKENS = 256
MEM_HEADS = 4
MEM_HEAD_DIM = 128
MEM_WIDTH = MEM_HEADS * MEM_HEAD_DIM
D_FF = 4 * D_MODEL
N_BRANCH = 3
LN_EPS = 1e-5
DN_ALPHA = (2 * DEPTH) ** 0.25
DN_BETA = (8 * DEPTH) ** -0.25
IN_SECTIONS = (A_WIDTH, A_WIDTH, A_WIDTH, IDX_HEADS * IDX_DIM, IDX_DIM, IDX_HEADS, S5_WIDTH, 2 * SG_WIDTH)
IN_WIDTH = sum(IN_SECTIONS)

kernel_name = 'hybrid_dsa_s5_sgmlp_deepnorm_step'


def split_in(proj):
    points = np.cumsum(np.array(IN_SECTIONS))[:-1].tolist()
    return jnp.split(proj, points, axis=-1)


def layer_norm(x, g, b):
    xf = x.astype(jnp.float32)
    mu = jnp.mean(xf, axis=-1, keepdims=True)
    var = jnp.mean(jnp.square(xf - mu), axis=-1, keepdims=True)
    y = (xf - mu) * lax.rsqrt(var + LN_EPS) * g.astype(jnp.float32) + b.astype(jnp.float32)
    return y.astype(x.dtype)


def rope(x, pos):
    half = x.shape[-1] // 2
    inv_freq = ROPE_THETA ** (-jnp.arange(half, dtype=jnp.float32) / half)
    ang = pos.astype(jnp.float32)[:, None] * inv_freq[None, :]
    cos = jnp.cos(ang)[:, None, :]
    sin = jnp.sin(ang)[:, None, :]
    xf = x.astype(jnp.float32)
    x1, x2 = xf[..., :half], xf[..., half:]
    return jnp.concatenate([x1 * cos - x2 * sin, x2 * cos + x1 * sin], axis=-1).astype(x.dtype)


def indexer_scores(q_idx, w_idx, k_idx):
    dots = jnp.einsum('bqhd,bsd->bqhs', q_idx.astype(jnp.float32), k_idx.astype(jnp.float32)) * (IDX_DIM ** -0.5)
    return jnp.einsum('bqh,bqhs->bqs', w_idx.astype(jnp.float32) * (IDX_HEADS ** -0.5), jax.nn.relu(dots))


def select_keys(scores, q_pos, k_sel):
    n_keys = scores.shape[-1]
    admissible = jnp.arange(n_keys)[None, None, :] <= q_pos[None, :, None]
    _, idx = lax.top_k(jnp.where(admissible, scores, -jnp.inf), k_sel)
    valid = idx <= q_pos[None, :, None]
    return idx, valid


def sparse_attend(q, k_sel, v_sel, valid):
    s = jnp.einsum('bqhd,bqkhd->bhqk', q, k_sel, preferred_element_type=jnp.float32) * (A_HEAD_DIM ** -0.5)
    s = jnp.where(valid[:, None], s, -jnp.inf)
    p = jax.nn.softmax(s, axis=-1).astype(v_sel.dtype)
    return jnp.einsum('bhqk,bqkhd->bqhd', p, v_sel)


def gather_rows(rows, idx):
    return jax.vmap(lambda r, i: r[i])(rows, idx)


def dsa_prompt(q, k, v, iq, iw, ik, k_sel):
    B, T = q.shape[:2]
    qb = min(Q_BLOCK, T)
    nb = T // qb
    pos = jnp.arange(T)

    def blocks(a):
        return a.reshape(B, nb, qb, *a.shape[2:]).swapaxes(0, 1)

    def one_block(args):
        q_b, iq_b, iw_b, pos_b = args
        idx, valid = select_keys(indexer_scores(iq_b, iw_b, ik), pos_b, k_sel)
        return sparse_attend(q_b, gather_rows(k, idx), gather_rows(v, idx), valid)

    out = lax.map(one_block, (blocks(q), blocks(iq), blocks(iw), pos.reshape(nb, qb)))
    return out.swapaxes(0, 1).reshape(B, T, A_HEADS, A_HEAD_DIM)


def dsa_sample(q, k, v, iq, iw, ik, cache_k, cache_v, cache_kidx, page_table, layer):
    DB, DS = q.shape[:2]
    n_pages = page_table.shape[1]
    past = n_pages * PAGE_SIZE
    ik_past = cache_kidx[layer, page_table].reshape(DB, past, IDX_DIM)
    ik_all = jnp.concatenate([ik_past.astype(ik.dtype), ik], axis=1)
    n_keys = past + DS
    k_sel = min(TOPK_MAX, n_keys // 4)
    q_pos = past + jnp.arange(DS)
    idx, valid = select_keys(indexer_scores(iq, iw, ik_all), q_pos, k_sel)
    in_past = idx < past
    pidx = jnp.minimum(idx, past - 1)
    phys = jax.vmap(lambda pt, i: pt[i])(page_table, pidx // PAGE_SIZE)
    off = pidx % PAGE_SIZE
    nidx = jnp.clip(idx - past, 0, DS - 1)
    sel = in_past[..., None, None]
    k_s = jnp.where(sel, cache_k[layer, phys, off].astype(k.dtype), gather_rows(k, nidx))
    v_s = jnp.where(sel, cache_v[layer, phys, off].astype(v.dtype), gather_rows(v, nidx))
    return sparse_attend(q, k_s, v_s, valid)


def s5_discretize(lam_re, lam_im, log_dt, b_re, b_im):
    f = jnp.float32
    lam_re = lam_re.astype(f)
    lam_im = lam_im.astype(f)
    dt = jnp.exp(log_dt.astype(f))[:, None]
    mag = jnp.exp(dt * lam_re)
    ang = dt * lam_im
    a_re = mag * jnp.cos(ang)
    a_im = mag * jnp.sin(ang)
    den = jnp.square(lam_re) + jnp.square(lam_im)
    c_re = ((a_re - 1.0) * lam_re + a_im * lam_im) / den
    c_im = (a_im * lam_re - (a_re - 1.0) * lam_im) / den
    b_re = b_re.astype(f)
    b_im = b_im.astype(f)
    bb_re = c_re[..., None] * b_re - c_im[..., None] * b_im
    bb_im = c_re[..., None] * b_im + c_im[..., None] * b_re
    return a_re, a_im, bb_re, bb_im


def complex_affine_combine(e1, e2):
    a1r, a1i, b1r, b1i = e1
    a2r, a2i, b2r, b2i = e2
    return (a1r * a2r - a1i * a2i,
            a1r * a2i + a1i * a2r,
            a2r * b1r - a2i * b1i + b2r,
            a2r * b1i + a2i * b1r + b2i)


def s5_scan(u, h0_re, h0_im, lam_re, lam_im, log_dt, b_re, b_im, c_re, c_im, d):
    f = jnp.float32
    B, T = u.shape[:2]
    a_re, a_im, bb_re, bb_im = s5_discretize(lam_re, lam_im, log_dt, b_re, b_im)
    uf = u.astype(f)
    bu_re = jnp.einsum('gph,btgh->btgp', bb_re, uf)
    bu_im = jnp.einsum('gph,btgh->btgp', bb_im, uf)
    h0_re = h0_re.astype(f)
    h0_im = h0_im.astype(f)
    bu_re = bu_re.at[:, 0].add(a_re * h0_re - a_im * h0_im)
    bu_im = bu_im.at[:, 0].add(a_re * h0_im + a_im * h0_re)
    ar = jnp.broadcast_to(a_re, bu_re.shape)
    ai = jnp.broadcast_to(a_im, bu_im.shape)
    _, _, h_re, h_im = lax.associative_scan(complex_affine_combine, (ar, ai, bu_re, bu_im), axis=1)
    y = (jnp.einsum('ghp,btgp->btgh', c_re.astype(f), h_re)
         - jnp.einsum('ghp,btgp->btgh', c_im.astype(f), h_im)
         + d.astype(f) * uf)
    return y.reshape(B, T, S5_WIDTH).astype(u.dtype), h_re[:, -1], h_im[:, -1]


def spatial_gate(z, ln_g, ln_b, w_sp, b_sp):
    B, T = z.shape[:2]
    z = jax.nn.gelu(z)
    u, v = jnp.split(z, 2, axis=-1)
    v = layer_norm(v, ln_g, ln_b)
    n = min(SG_CHUNK, T)
    vc = v.reshape(B, T // n, n, SG_GROUPS, SG_GROUP_CH)
    causal = jnp.tril(jnp.ones((n, n), dtype=bool))
    w = jnp.where(causal, w_sp[:, :n, :n], 0.0).astype(v.dtype)
    s = jnp.einsum('gij,bcjgd->bcigd', w, vc) + b_sp[:, :n].T[:, :, None].astype(v.dtype)
    return u * s.reshape(B, T, SG_WIDTH), v


def memory_attend(x, mk, mv, wq, wo):
    B, T = x.shape[:2]
    q = (x @ wq).reshape(B, T, MEM_HEADS, MEM_HEAD_DIM)
    s = jnp.einsum('bqhd,bmhd->bhqm', q, mk.astype(q.dtype), preferred_element_type=jnp.float32) * (MEM_HEAD_DIM ** -0.5)
    p = jax.nn.softmax(s, axis=-1).astype(q.dtype)
    o = jnp.einsum('bhqm,bmhd->bqhd', p, mv.astype(q.dtype)).reshape(B, T, MEM_WIDTH)
    return o @ wo


def setup_inputs(seed: int = 0) -> dict:
    key = jax.random.key(seed)
    keys = iter(jax.random.split(key, 64))

    def nrm(shape, scale):
        return jax.random.normal(next(keys), shape, jnp.float32) * scale

    n_pages = PAST_LEN // PAGE_SIZE
    n_used = DEC_BATCH * n_pages
    n_pool = n_used + (n_used + 3) // 4
    L = DEPTH
    D = D_MODEL
    inp = {}
    inp['x_prompt'] = nrm((BATCH, SEQ, D), 1.0)
    inp['x_sample'] = nrm((DEC_BATCH, DEC_SEQ, D), 1.0)
    inp['cache_k'] = nrm((L, n_pool, PAGE_SIZE, A_HEADS, A_HEAD_DIM), 1.0)
    inp['cache_v'] = nrm((L, n_pool, PAGE_SIZE, A_HEADS, A_HEAD_DIM), DN_BETA)
    inp['cache_kidx'] = nrm((L, n_pool, PAGE_SIZE, IDX_DIM), 1.0)
    inp['cache_mem_k'] = nrm((L, DEC_BATCH, MEM_TOKENS, MEM_HEADS, MEM_HEAD_DIM), 1.0)
    inp['cache_mem_v'] = nrm((L, DEC_BATCH, MEM_TOKENS, MEM_HEADS, MEM_HEAD_DIM), DN_BETA)
    inp['state_s5_re'] = nrm((L, DEC_BATCH, S5_GROUPS, S5_STATE), 0.1)
    inp['state_s5_im'] = nrm((L, DEC_BATCH, S5_GROUPS, S5_STATE), 0.1)
    inp['page_table'] = jax.random.permutation(next(keys), n_pool)[:n_used].reshape(DEC_BATCH, n_pages).astype(jnp.int32)
    inp['mem_prompt'] = nrm((BATCH, MEM_TOKENS, D), 1.0)
    w_in = nrm((L, D, IN_WIDTH), D ** -0.5)
    inp['w_in'] = w_in.at[:, :, 2 * A_WIDTH:3 * A_WIDTH].multiply(DN_BETA)
    inp['w_a_out'] = nrm((L, A_WIDTH, D), A_WIDTH ** -0.5)
    inp['s5_lambda_re'] = -0.5 + nrm((L, S5_GROUPS, S5_STATE), 0.01)
    inp['s5_lambda_im'] = math.pi * jnp.arange(S5_STATE, dtype=jnp.float32) + nrm((L, S5_GROUPS, S5_STATE), 0.01)
    inp['s5_log_dt'] = jax.random.uniform(next(keys), (L, S5_GROUPS), jnp.float32, math.log(DT_MIN), math.log(DT_MAX))
    inp['s5_b_re'] = nrm((L, S5_GROUPS, S5_STATE, S5_GROUP_CH), (2 * S5_GROUP_CH) ** -0.5)
    inp['s5_b_im'] = nrm((L, S5_GROUPS, S5_STATE, S5_GROUP_CH), (2 * S5_GROUP_CH) ** -0.5)
    inp['s5_c_re'] = nrm((L, S5_GROUPS, S5_GROUP_CH, S5_STATE), S5_STATE ** -0.5)
    inp['s5_c_im'] = nrm((L, S5_GROUPS, S5_GROUP_CH, S5_STATE), S5_STATE ** -0.5)
    inp['s5_d'] = nrm((L, S5_GROUPS, S5_GROUP_CH), 1.0)
    inp['w_glu_a'] = nrm((L, S5_WIDTH, D), S5_WIDTH ** -0.5)
    inp['w_glu_b'] = nrm((L, S5_WIDTH, D), S5_WIDTH ** -0.5)
    inp['sg_ln_g'] = 1.0 + nrm((L, SG_WIDTH), 0.01)
    inp['sg_ln_b'] = nrm((L, SG_WIDTH), 0.01)
    inp['sg_w'] = nrm((L, SG_GROUPS, SG_CHUNK, SG_CHUNK), SG_CHUNK ** -0.5)
    inp['sg_b'] = 1.0 + nrm((L, SG_GROUPS, SG_CHUNK), 0.01)
    inp['w_c_out'] = nrm((L, SG_WIDTH, D), SG_WIDTH ** -0.5)
    inp['w_gate'] = nrm((L, D, N_BRANCH * D), D ** -0.5)
    inp['b_gate'] = nrm((L, N_BRANCH * D), 0.01)
    inp['w_mix_out'] = nrm((L, D, D), D ** -0.5 * DN_BETA)
    inp['ln1_g'] = 1.0 + nrm((L, D), 0.01)
    inp['ln1_b'] = nrm((L, D), 0.01)
    inp['w_mq'] = nrm((L, D, MEM_WIDTH), D ** -0.5)
    inp['w_mk'] = nrm((L, D, MEM_WIDTH), D ** -0.5)
    inp['w_mv'] = nrm((L, D, MEM_WIDTH), D ** -0.5 * DN_BETA)
    inp['w_mo'] = nrm((L, MEM_WIDTH, D), MEM_WIDTH ** -0.5 * DN_BETA)
    inp['ln2_g'] = 1.0 + nrm((L, D), 0.01)
    inp['ln2_b'] = nrm((L, D), 0.01)
    inp['w_ff1'] = nrm((L, D, D_FF), D ** -0.5)
    inp['w_ff2'] = nrm((L, D_FF, D), D_FF ** -0.5 * DN_BETA)
    inp['ln3_g'] = 1.0 + nrm((L, D), 0.01)
    inp['ln3_b'] = nrm((L, D), 0.01)
    return inp


def reference(x_prompt, x_sample, cache_k, cache_v, cache_kidx, cache_mem_k, cache_mem_v, state_s5_re, state_s5_im,
              page_table, mem_prompt, w_in, w_a_out, s5_lambda_re, s5_lambda_im, s5_log_dt, s5_b_re, s5_b_im,
              s5_c_re, s5_c_im, s5_d, w_glu_a, w_glu_b, sg_ln_g, sg_ln_b, sg_w, sg_b, w_c_out, w_gate, b_gate,
              w_mix_out, ln1_g, ln1_b, w_mq, w_mk, w_mv, w_mo, ln2_g, ln2_b, w_ff1, w_ff2, ln3_g, ln3_b):

    def token_mix(x, l, pos, attend, h0_re, h0_im):
        B, T = x.shape[:2]
        q, k, v, iq, ik, iw, u5, zc = split_in(x @ w_in[l])
        q = rope(q.reshape(B, T, A_HEADS, A_HEAD_DIM), pos)
        k = rope(k.reshape(B, T, A_HEADS, A_HEAD_DIM), pos)
        v = v.reshape(B, T, A_HEADS, A_HEAD_DIM)
        iq = rope(iq.reshape(B, T, IDX_HEADS, IDX_DIM), pos)
        ik = rope(ik.reshape(B, T, 1, IDX_DIM), pos)[:, :, 0]
        att = attend(q, k, v, iq, iw, ik)
        br_a = att.reshape(B, T, A_WIDTH) @ w_a_out[l]
        y5, h_re, h_im = s5_scan(u5.reshape(B, T, S5_GROUPS, S5_GROUP_CH), h0_re, h0_im, s5_lambda_re[l],
                                 s5_lambda_im[l], s5_log_dt[l], s5_b_re[l], s5_b_im[l], s5_c_re[l], s5_c_im[l], s5_d[l])
        y5 = jax.nn.gelu(y5)
        br_b = (y5 @ w_glu_a[l]) * jax.nn.sigmoid(y5 @ w_glu_b[l])
        sg, v_sg = spatial_gate(zc, sg_ln_g[l], sg_ln_b[l], sg_w[l], sg_b[l])
        br_c = sg @ w_c_out[l]
        g = jax.nn.sigmoid(x @ w_gate[l] + b_gate[l])
        g_a, g_b, g_c = jnp.split(g, N_BRANCH, axis=-1)
        merged = g_a * br_a + g_b * br_b + g_c * br_c
        x = layer_norm(DN_ALPHA * x + merged @ w_mix_out[l], ln1_g[l], ln1_b[l])
        return x, k, v, ik, h_re, h_im, v_sg

    def memory_block(x, l, mk, mv):
        return layer_norm(DN_ALPHA * x + memory_attend(x, mk, mv, w_mq[l], w_mo[l]), ln2_g[l], ln2_b[l])

    def channel_block(x, l):
        hid = jnp.square(jax.nn.relu(x @ w_ff1[l]))
        return layer_norm(DN_ALPHA * x + hid @ w_ff2[l], ln3_g[l], ln3_b[l])

    Bp, Tp = x_prompt.shape[:2]
    pos_p = jnp.arange(Tp)
    attend_p = functools.partial(dsa_prompt, k_sel=min(TOPK_MAX, Tp // 4))
    h0 = jnp.zeros((Bp, S5_GROUPS, S5_STATE), jnp.float32)
    h = x_prompt
    kp, vp, ikp, mkp, mvp, s5rp, s5ip = [], [], [], [], [], [], []
    for l in range(DEPTH):
        h, k, v, ik, h_re, h_im, _ = token_mix(h, l, pos_p, attend_p, h0, h0)
        mk = (mem_prompt @ w_mk[l]).reshape(Bp, MEM_TOKENS, MEM_HEADS, MEM_HEAD_DIM)
        mv = (mem_prompt @ w_mv[l]).reshape(Bp, MEM_TOKENS, MEM_HEADS, MEM_HEAD_DIM)
        h = memory_block(h, l, mk, mv)
        h = channel_block(h, l)
        kp.append(k)
        vp.append(v)
        ikp.append(ik)
        mkp.append(mk)
        mvp.append(mv)
        s5rp.append(h_re)
        s5ip.append(h_im)
    y_prompt = h

    pos_s = PAST_LEN + jnp.arange(x_sample.shape[1])
    h = x_sample
    ks, vs, iks, s5rs, s5is, sgs = [], [], [], [], [], []
    for l in range(DEPTH):
        attend_s = functools.partial(dsa_sample, cache_k=cache_k, cache_v=cache_v, cache_kidx=cache_kidx,
                                     page_table=page_table, layer=l)
        h, k, v, ik, h_re, h_im, v_sg = token_mix(h, l, pos_s, attend_s, state_s5_re[l], state_s5_im[l])
        h = memory_block(h, l, cache_mem_k[l], cache_mem_v[l])
        h = channel_block(h, l)
        ks.append(k)
        vs.append(v)
        iks.append(ik)
        s5rs.append(h_re)
        s5is.append(h_im)
        sgs.append(v_sg)
    y_sample = h

    return (y_prompt, y_sample,
            jnp.stack(kp), jnp.stack(vp), jnp.stack(ikp), jnp.stack(mkp), jnp.stack(mvp),
            jnp.stack(s5rp), jnp.stack(s5ip),
            jnp.stack(ks), jnp.stack(vs), jnp.stack(iks), jnp.stack(s5rs), jnp.stack(s5is), jnp.stack(sgs))
```

```python
import functools
import math

import numpy as np
import jax
import jax.numpy as jnp
from jax import lax
from jax.experimental import pallas as pl
from jax.experimental.pallas import tpu as pltpu

F32 = jnp.float32
BF16 = jnp.bfloat16
I32 = jnp.int32

D_MODEL = 1024
A_HEADS = 8
A_HEAD_DIM = 64
A_WIDTH = A_HEADS * A_HEAD_DIM
IDX_HEADS = 4
IDX_DIM = 64
TOPK_MAX = 256
ROPE_THETA = 10000.0
S5_GROUP_CH = 16
S5_GROUPS = 16
S5_WIDTH = S5_GROUPS * S5_GROUP_CH
S5_STATE = 64
S5_FLAT = S5_GROUPS * S5_STATE
SG_CHUNK = 128
SG_GROUPS = 4
SG_WIDTH = 256
SG_GROUP_CH = SG_WIDTH // SG_GROUPS
MEM_TOKENS = 256
MEM_HEADS = 4
MEM_HEAD_DIM = 128
MEM_WIDTH = MEM_HEADS * MEM_HEAD_DIM
D_FF = 4 * D_MODEL
N_BRANCH = 3
LN_EPS = 1e-5
PAGE_SIZE = 128

LANES = 128
SUBLANES = 8
VMEM_PHYSICAL_BYTES = 64 * 1024 * 1024

IKW_PAD = LANES - IDX_DIM - IDX_HEADS
OFF_Q = 0
OFF_K = OFF_Q + A_WIDTH
OFF_V = OFF_K + A_WIDTH
OFF_IQ = OFF_V + A_WIDTH
OFF_IKW = OFF_IQ + IDX_HEADS * IDX_DIM
OFF_U5 = OFF_IKW + LANES
OFF_ZC = OFF_U5 + S5_WIDTH
IN_PACKED = OFF_ZC + 2 * SG_WIDTH

INT_MIN = -2 ** 31
NEG_INF_KEY = int(np.array([0xFF800000 ^ 0x7FFFFFFF], dtype=np.uint32).view(np.int32)[0])


def _cparams(semantics, vmem_bytes):
    assert vmem_bytes < VMEM_PHYSICAL_BYTES
    return pltpu.CompilerParams(dimension_semantics=semantics, vmem_limit_bytes=int(vmem_bytes))


def _mib(n):
    return n * 1024 * 1024


def _const_spec(shape):
    n = len(shape)
    return pl.BlockSpec(shape, lambda *_: (0,) * n)


def _layer_norm(x, g, b):
    mu = jnp.mean(x, axis=-1, keepdims=True)
    xc = x - mu
    var = jnp.mean(xc * xc, axis=-1, keepdims=True)
    return xc * lax.rsqrt(var + LN_EPS) * g + b


def _dot(a, b):
    return jnp.dot(a.astype(BF16), b.astype(BF16), preferred_element_type=F32)


def _dot_nt(a, b):
    return lax.dot_general(a.astype(BF16), b.astype(BF16), (((1,), (1,)), ((), ())),
                           preferred_element_type=F32)


def _proj_kernel(x_ref, w_ref, cos_ref, sin_ref, q_ref, k_ref, v_ref, iq_ref, ikw_ref, u5_ref, zc_ref,
                 kbf_ref, vbf_ref):
    y = _dot(x_ref[...], w_ref[...])
    cos = cos_ref[...]
    sin = sin_ref[...]
    lane = lax.broadcasted_iota(I32, cos.shape, 1)
    first_half = (lane & (A_HEAD_DIM // 2)) == 0

    def rope(t, c, s):
        partner = jnp.where(first_half, pltpu.roll(t, LANES - A_HEAD_DIM // 2, 1), pltpu.roll(t, A_HEAD_DIM // 2, 1))
        return t * c + partner * s

    for g in range(A_WIDTH // LANES):
        sl = slice(g * LANES, (g + 1) * LANES)
        q_ref[:, sl] = rope(y[:, OFF_Q + g * LANES:OFF_Q + (g + 1) * LANES], cos, sin)
        kg = rope(y[:, OFF_K + g * LANES:OFF_K + (g + 1) * LANES], cos, sin)
        k_ref[:, sl] = kg
        kbf_ref[:, sl] = kg.astype(BF16)
    v = y[:, OFF_V:OFF_V + A_WIDTH]
    v_ref[...] = v
    vbf_ref[...] = v.astype(BF16)
    for g in range(IDX_HEADS * IDX_DIM // LANES):
        iq_ref[:, g * LANES:(g + 1) * LANES] = rope(y[:, OFF_IQ + g * LANES:OFF_IQ + (g + 1) * LANES], cos, sin)
    is_key = lane < IDX_DIM
    ikw_ref[...] = rope(y[:, OFF_IKW:OFF_IKW + LANES], jnp.where(is_key, cos, 1.0), jnp.where(is_key, sin, 0.0))
    u5_ref[...] = y[:, OFF_U5:OFF_U5 + S5_WIDTH]
    zc_ref[...] = y[:, OFF_ZC:OFF_ZC + 2 * SG_WIDTH]


def _proj(x, w, cos, sin, tm):
    n = x.shape[0]
    n_tab = cos.shape[0] // tm
    row = lambda w_: pl.BlockSpec((tm, w_), lambda i: (i, 0))
    tab = pl.BlockSpec((tm, LANES), lambda i: (i % n_tab, 0))
    widths = (A_WIDTH, A_WIDTH, A_WIDTH, IDX_HEADS * IDX_DIM, LANES, S5_WIDTH, 2 * SG_WIDTH)
    out_shape = [jax.ShapeDtypeStruct((n, w_), F32) for w_ in widths] + [jax.ShapeDtypeStruct((n, A_WIDTH), BF16)] * 2
    out_specs = [row(w_) for w_ in widths] + [row(A_WIDTH)] * 2
    return pl.pallas_call(
        _proj_kernel, grid=(n // tm,),
        in_specs=[row(D_MODEL), _const_spec((D_MODEL, IN_PACKED)), tab, tab],
        out_specs=out_specs, out_shape=out_shape,
        compiler_params=_cparams(("parallel",), _mib(40)), name="in_proj_rope",
    )(x, w, cos, sin)


def _score_keys(score):
    score = jnp.where(score == 0.0, 0.0, score)
    bits = lax.bitcast_convert_type(score, I32)
    return jnp.where(bits < 0, bits ^ 0x7FFFFFFF, bits)


def _count(key_ref, n_cols, pred, extra=None):
    rows = key_ref.shape[0]
    acc = jnp.zeros((rows, LANES), I32)
    for c in range(n_cols // LANES):
        acc = acc + pred(key_ref[:, c * LANES:(c + 1) * LANES], c * LANES).astype(I32)
    cnt = jnp.sum(acc, axis=1, keepdims=True)
    if extra is not None:
        cnt = cnt + extra.astype(I32)
    return cnt


def _topk_threshold(key_ref, n_cols, k_sel, xkey=None):
    rows = key_ref.shape[0]

    def bit_body(i, prefix):
        cand = prefix | jnp.left_shift(jnp.int32(1), 31 - i)
        cand_s = cand ^ INT_MIN
        cnt = _count(key_ref, n_cols, lambda ch, _: ch >= cand_s, None if xkey is None else xkey >= cand_s)
        return jnp.where(cnt >= k_sel, cand, prefix)

    prefix = lax.fori_loop(0, 32, bit_body, jnp.zeros((rows, 1), I32))
    thr = prefix ^ INT_MIN
    cnt_gt = _count(key_ref, n_cols, lambda ch, _: ch > thr, None if xkey is None else xkey > thr)
    cnt_eq = _count(key_ref, n_cols, lambda ch, _: ch == thr, None if xkey is None else xkey == thr)
    need = k_sel - cnt_gt
    last_col = n_cols - 1 if xkey is None else n_cols
    surplus = jnp.logical_and(cnt_eq > need, thr > NEG_INF_KEY)
    any_surplus = jnp.max(surplus.astype(I32)) > 0
    lane = lax.broadcasted_iota(I32, (rows, LANES), 1)

    def tie_search():
        def body(i, lo):
            cand = lo + jnp.left_shift(jnp.int32(1), last_col.bit_length() - 1 - i)

            def pred(ch, c0):
                return jnp.where(ch == thr, (lane <= cand - c0).astype(I32), 0)

            extra = None if xkey is None else jnp.logical_and(xkey == thr, cand >= n_cols)
            cnt = _count(key_ref, n_cols, pred, extra)
            return jnp.where(cnt < need, cand, lo)

        lo = lax.fori_loop(0, last_col.bit_length(), body, jnp.full((rows, 1), -1, I32))
        return lo + 1

    jstar = lax.cond(any_surplus, tie_search, lambda: jnp.full((rows, 1), last_col, I32))
    return thr, jstar


def _selected(ch, col, thr, jstar):
    return jnp.where(ch == thr, (col <= jstar).astype(I32), (ch > thr).astype(I32)) > 0


def _dsa_prompt_block(n_keys, k_sel, qb, q_ref, iq_ref, iwq_ref, ikw_ref, kbf_ref, vbf_ref, o_ref, key_scr, bias_scr):
    q0 = pl.program_id(1) * qb
    ik = ikw_ref[0, :n_keys, :IDX_DIM].astype(BF16)
    iq = iq_ref[0]
    iw = iwq_ref[0][:, IDX_DIM:IDX_DIM + IDX_HEADS] * (IDX_HEADS ** -0.5)
    score = jnp.zeros((qb, n_keys), F32)
    for h in range(IDX_HEADS):
        dots = _dot_nt(iq[:, h * IDX_DIM:(h + 1) * IDX_DIM], ik) * (IDX_DIM ** -0.5)
        score = score + iw[:, h:h + 1] * jnp.maximum(dots, 0.0)
    row = q0 + lax.broadcasted_iota(I32, (qb, n_keys), 0)
    col = lax.broadcasted_iota(I32, (qb, n_keys), 1)
    key_scr[:, :n_keys] = _score_keys(jnp.where(col <= row, score, -jnp.inf))

    thr, jstar = _topk_threshold(key_scr, n_keys, k_sel)
    row_c = q0 + lax.broadcasted_iota(I32, (qb, LANES), 0)
    lane = lax.broadcasted_iota(I32, (qb, LANES), 1)
    for c in range(n_keys // LANES):
        colc = lane + c * LANES
        sel = _selected(key_scr[:, c * LANES:(c + 1) * LANES], colc, thr, jstar)
        bias_scr[:, c * LANES:(c + 1) * LANES] = jnp.where(colc <= row_c, jnp.where(sel, 0.0, -jnp.inf), -jnp.inf)

    q = q_ref[0]
    bias = bias_scr[:, :n_keys]
    low = lane < A_HEAD_DIM
    for g in range(A_WIDTH // LANES):
        sl = slice(g * LANES, (g + 1) * LANES)
        qg = q[:, sl]
        kg = kbf_ref[0, :n_keys, sl]
        vg = vbf_ref[0, :n_keys, sl]
        outs = []
        for first in (True, False):
            qh = jnp.where(low == first, qg, 0.0)
            s = _dot_nt(qh, kg) * (A_HEAD_DIM ** -0.5) + bias
            m = jnp.max(s, axis=1, keepdims=True)
            p = jnp.exp(s - m)
            l = jnp.sum(p, axis=1, keepdims=True)
            outs.append(_dot(p, vg) / l)
        o_ref[0, :, sl] = jnp.where(low, outs[0], outs[1])


def _dsa_prompt_kernel(k_sel, qb, key_step, q_ref, iq_ref, iwq_ref, ikw_ref, kbf_ref, vbf_ref, o_ref, key_scr, bias_scr):
    t = ikw_ref.shape[1]
    cls = (pl.program_id(1) * qb + qb - 1) // key_step
    for c in range(t // key_step):
        @pl.when(cls == c)
        def _(c=c):
            _dsa_prompt_block((c + 1) * key_step, k_sel, qb, q_ref, iq_ref, iwq_ref, ikw_ref, kbf_ref, vbf_ref, o_ref,
                              key_scr, bias_scr)


def _dsa_prompt(q, iq, ikw, kbf, vbf, k_sel, qb, key_step):
    b, t, _ = q.shape
    blk = lambda w_: pl.BlockSpec((1, qb, w_), lambda i, j: (i, j, 0))
    full = lambda w_: pl.BlockSpec((1, t, w_), lambda i, j: (i, 0, 0))
    return pl.pallas_call(
        functools.partial(_dsa_prompt_kernel, k_sel, qb, key_step), grid=(b, t // qb),
        in_specs=[blk(A_WIDTH), blk(IDX_HEADS * IDX_DIM), blk(LANES), full(LANES), full(A_WIDTH), full(A_WIDTH)],
        out_specs=blk(A_WIDTH), out_shape=jax.ShapeDtypeStruct((b, t, A_WIDTH), F32),
        scratch_shapes=[pltpu.VMEM((qb, t), I32), pltpu.VMEM((qb, t), F32)],
        compiler_params=_cparams(("parallel", "arbitrary"), _mib(56)), name="dsa_prompt",
    )(q, iq, ikw, ikw, kbf, vbf)


S5_SEG = 64
S5_CHUNK = S5_SEG * SUBLANES


def _s5_prompt_kernel(u_ref, bblk_ref, cblk_ref, d_ref, a_ref, h0_ref, y_ref, hout_ref,
                      h_scr, pw_scr, hin_scr, carry_scr):
    c = pl.program_id(1)
    a = a_ref[...]
    a_re1, a_im1 = a[:, :S5_FLAT], a[:, S5_FLAT:]

    @pl.when(c == 0)
    def _():
        carry_scr[...] = h0_ref[0]
        pr, pi = a_re1, a_im1
        for k in range(S5_SEG):
            pw_scr[k:k + 1, :S5_FLAT] = pr
            pw_scr[k:k + 1, S5_FLAT:] = pi
            pr, pi = pr * a_re1 - pi * a_im1, pr * a_im1 + pi * a_re1

    up = u_ref[0]
    h_scr[...] = _dot(up, bblk_ref[...])

    a_re = jnp.broadcast_to(a_re1, (SUBLANES, S5_FLAT))
    a_im = jnp.broadcast_to(a_im1, (SUBLANES, S5_FLAT))

    def scan_step(k, carry):
        hr, hi = carry
        rows = pl.ds(pl.multiple_of(k * SUBLANES, SUBLANES), SUBLANES)
        nhr = a_re * hr - a_im * hi + h_scr[rows, :S5_FLAT]
        nhi = a_re * hi + a_im * hr + h_scr[rows, S5_FLAT:]
        h_scr[rows, :S5_FLAT] = nhr
        h_scr[rows, S5_FLAT:] = nhi
        return nhr, nhi

    zero = jnp.zeros((SUBLANES, S5_FLAT), F32)
    end_re, end_im = lax.fori_loop(0, S5_SEG, scan_step, (zero, zero))

    p_last_re, p_last_im = pw_scr[S5_SEG - 1:S5_SEG, :S5_FLAT], pw_scr[S5_SEG - 1:S5_SEG, S5_FLAT:]
    cr, ci = carry_scr[:, :S5_FLAT], carry_scr[:, S5_FLAT:]
    for j in range(SUBLANES):
        hin_scr[j:j + 1, :S5_FLAT] = cr
        hin_scr[j:j + 1, S5_FLAT:] = ci
        cr, ci = (end_re[j:j + 1] + p_last_re * cr - p_last_im * ci,
                  end_im[j:j + 1] + p_last_re * ci + p_last_im * cr)
    carry_scr[:, :S5_FLAT] = cr
    carry_scr[:, S5_FLAT:] = ci
    hin_re, hin_im = hin_scr[:, :S5_FLAT], hin_scr[:, S5_FLAT:]

    def fix_step(k, _):
        rows = pl.ds(pl.multiple_of(k * SUBLANES, SUBLANES), SUBLANES)
        pr = jnp.broadcast_to(pw_scr[pl.ds(k, 1), :S5_FLAT], (SUBLANES, S5_FLAT))
        pi = jnp.broadcast_to(pw_scr[pl.ds(k, 1), S5_FLAT:], (SUBLANES, S5_FLAT))
        h_scr[rows, :S5_FLAT] = h_scr[rows, :S5_FLAT] + pr * hin_re - pi * hin_im
        h_scr[rows, S5_FLAT:] = h_scr[rows, S5_FLAT:] + pr * hin_im + pi * hin_re
        return 0

    lax.fori_loop(0, S5_SEG, fix_step, 0)
    y_ref[0] = _dot(h_scr[...], cblk_ref[...]) + d_ref[...] * up

    @pl.when(c == pl.num_programs(1) - 1)
    def _():
        hout_ref[0] = carry_scr[...]


def _s5_prompt(u, bblk, cblk, dvec, avec, h0):
    b, t, _ = u.shape
    blk = pl.BlockSpec((1, S5_CHUNK, S5_WIDTH), lambda i, j: (i, j, 0))
    st = pl.BlockSpec((1, 1, 2 * S5_FLAT), lambda i, j: (i, 0, 0))
    return pl.pallas_call(
        _s5_prompt_kernel, grid=(b, t // S5_CHUNK),
        in_specs=[blk, _const_spec((S5_WIDTH, 2 * S5_FLAT)), _const_spec((2 * S5_FLAT, S5_WIDTH)),
                  _const_spec((1, S5_WIDTH)), _const_spec((1, 2 * S5_FLAT)), st],
        out_specs=[blk, st],
        out_shape=[jax.ShapeDtypeStruct((b, t, S5_WIDTH), F32), jax.ShapeDtypeStruct((b, 1, 2 * S5_FLAT), F32)],
        scratch_shapes=[pltpu.VMEM((S5_CHUNK, 2 * S5_FLAT), F32),
                        pltpu.VMEM((S5_SEG, 2 * S5_FLAT), F32), pltpu.VMEM((SUBLANES, 2 * S5_FLAT), F32),
                        pltpu.VMEM((1, 2 * S5_FLAT), F32)],
        compiler_params=_cparams(("parallel", "arbitrary"), _mib(32)), name="s5_prompt",
    )(u, bblk, cblk, dvec, avec, h0)


def _sgate_prompt_kernel(zc_ref, g_ref, b_ref, w_ref, bias_ref, o_ref):
    z = jax.nn.gelu(zc_ref[...])
    u = z[:, :SG_WIDTH]
    v = _layer_norm(z[:, SG_WIDTH:], g_ref[...], b_ref[...])
    ri = lax.broadcasted_iota(I32, (SG_CHUNK, SG_CHUNK), 0)
    ci = lax.broadcasted_iota(I32, (SG_CHUNK, SG_CHUNK), 1)
    wm = [jnp.where(ci <= ri, w_ref[g], 0.0).astype(BF16) for g in range(SG_GROUPS)]
    low = lax.broadcasted_iota(I32, (SG_CHUNK, LANES), 1) < SG_GROUP_CH
    bias = bias_ref[...]
    for c in range(zc_ref.shape[0] // SG_CHUNK):
        rows = slice(c * SG_CHUNK, (c + 1) * SG_CHUNK)
        for pr in range(SG_WIDTH // LANES):
            cols = slice(pr * LANES, (pr + 1) * LANES)
            vp = v[rows, cols].astype(BF16)
            s = jnp.where(low, _dot(wm[2 * pr], vp), _dot(wm[2 * pr + 1], vp)) + bias[:, cols]
            o_ref[rows, cols] = u[rows, cols] * s


def _sgate_prompt(zc, g, b, w, bias, tm):
    n = zc.shape[0]
    return pl.pallas_call(
        _sgate_prompt_kernel, grid=(n // tm,),
        in_specs=[pl.BlockSpec((tm, 2 * SG_WIDTH), lambda i: (i, 0)), _const_spec((1, SG_WIDTH)),
                  _const_spec((1, SG_WIDTH)), _const_spec((SG_GROUPS, SG_CHUNK, SG_CHUNK)),
                  _const_spec((SG_CHUNK, SG_WIDTH))],
        out_specs=pl.BlockSpec((tm, SG_WIDTH), lambda i: (i, 0)),
        out_shape=jax.ShapeDtypeStruct((n, SG_WIDTH), F32),
        compiler_params=_cparams(("parallel",), _mib(24)), name="sgate_prompt",
    )(zc, g, b, w, bias)


def _merge_kernel(alpha, x_ref, att_ref, y5_ref, sg_ref, wa_ref, wga_ref, wgb_ref, wc_ref, wg_ref, bg_ref, wmix_ref,
                  g_ref, b_ref, o_ref):
    x = x_ref[...]
    gates = jax.nn.sigmoid(_dot(x, wg_ref[...]) + bg_ref[...])
    br_a = _dot(att_ref[...], wa_ref[...])
    y5 = jax.nn.gelu(y5_ref[...])
    br_b = _dot(y5, wga_ref[...]) * jax.nn.sigmoid(_dot(y5, wgb_ref[...]))
    br_c = _dot(sg_ref[...], wc_ref[...])
    merged = (gates[:, :D_MODEL] * br_a + gates[:, D_MODEL:2 * D_MODEL] * br_b + gates[:, 2 * D_MODEL:] * br_c)
    o_ref[...] = _layer_norm(alpha * x + _dot(merged, wmix_ref[...]), g_ref[...], b_ref[...])


def _merge(alpha, x, att, y5, sg, wa, wga, wgb, wc, wg, bg, wmix, g, b, tm):
    n = x.shape[0]
    row = lambda w_: pl.BlockSpec((tm, w_), lambda i: (i, 0))
    return pl.pallas_call(
        functools.partial(_merge_kernel, alpha), grid=(n // tm,),
        in_specs=[row(D_MODEL), row(A_WIDTH), row(S5_WIDTH), row(SG_WIDTH),
                  _const_spec((A_WIDTH, D_MODEL)), _const_spec((S5_WIDTH, D_MODEL)), _const_spec((S5_WIDTH, D_MODEL)),
                  _const_spec((SG_WIDTH, D_MODEL)), _const_spec((D_MODEL, N_BRANCH * D_MODEL)),
                  _const_spec((1, N_BRANCH * D_MODEL)), _const_spec((D_MODEL, D_MODEL)),
                  _const_spec((1, D_MODEL)), _const_spec((1, D_MODEL))],
        out_specs=row(D_MODEL), out_shape=jax.ShapeDtypeStruct((n, D_MODEL), F32),
        compiler_params=_cparams(("parallel",), _mib(48)), name="branch_merge",
    )(x, att, y5, sg, wa, wga, wgb, wc, wg, bg, wmix, g, b)


def _two_linear_kernel(x_ref, w1_ref, w2_ref, o1_ref, o2_ref):
    x = x_ref[...]
    o1_ref[...] = _dot(x, w1_ref[...])
    o2_ref[...] = _dot(x, w2_ref[...])


def _two_linear(x, w1, w2, tm):
    n, k = x.shape
    m = w1.shape[1]
    return pl.pallas_call(
        _two_linear_kernel, grid=(n // tm,),
        in_specs=[pl.BlockSpec((tm, k), lambda i: (i, 0)), _const_spec((k, m)), _const_spec((k, m))],
        out_specs=[pl.BlockSpec((tm, m), lambda i: (i, 0))] * 2,
        out_shape=[jax.ShapeDtypeStruct((n, m), F32)] * 2,
        compiler_params=_cparams(("parallel",), _mib(24)), name="mem_kv_proj",
    )(x, w1, w2)


def _mem_prompt_kernel(alpha, x_ref, mk_ref, mv_ref, wq_ref, wo_ref, g_ref, b_ref, o_ref):
    x = x_ref[0]
    q = _dot(x, wq_ref[...])
    mk = mk_ref[0].astype(BF16)
    mv = mv_ref[0].astype(BF16)
    outs = []
    for h in range(MEM_HEADS):
        sl = slice(h * MEM_HEAD_DIM, (h + 1) * MEM_HEAD_DIM)
        s = _dot_nt(q[:, sl], mk[:, sl]) * (MEM_HEAD_DIM ** -0.5)
        m = jnp.max(s, axis=1, keepdims=True)
        p = jnp.exp(s - m)
        l = jnp.sum(p, axis=1, keepdims=True)
        outs.append(_dot(p, mv[:, sl]) / l)
    o = jnp.concatenate(outs, axis=1)
    o_ref[0] = _layer_norm(alpha * x + _dot(o, wo_ref[...]), g_ref[...], b_ref[...])


def _mem_prompt(alpha, x, mk, mv, wq, wo, g, b, tm):
    bsz, t, _ = x.shape
    return pl.pallas_call(
        functools.partial(_mem_prompt_kernel, alpha), grid=(bsz, t // tm),
        in_specs=[pl.BlockSpec((1, tm, D_MODEL), lambda i, j: (i, j, 0)),
                  pl.BlockSpec((1, MEM_TOKENS, MEM_WIDTH), lambda i, j: (i, 0, 0)),
                  pl.BlockSpec((1, MEM_TOKENS, MEM_WIDTH), lambda i, j: (i, 0, 0)),
                  _const_spec((D_MODEL, MEM_WIDTH)), _const_spec((MEM_WIDTH, D_MODEL)),
                  _const_spec((1, D_MODEL)), _const_spec((1, D_MODEL))],
        out_specs=pl.BlockSpec((1, tm, D_MODEL), lambda i, j: (i, j, 0)),
        out_shape=jax.ShapeDtypeStruct((bsz, t, D_MODEL), F32),
        compiler_params=_cparams(("parallel", "parallel"), _mib(32)), name="mem_attend_prompt",
    )(x, mk, mv, wq, wo, g, b)


def _ffn_kernel(alpha, x_ref, w1_ref, w2_ref, g_ref, b_ref, o_ref):
    x = x_ref[...]
    hid = jnp.square(jnp.maximum(_dot(x, w1_ref[...]), 0.0))
    o_ref[...] = _layer_norm(alpha * x + _dot(hid, w2_ref[...]), g_ref[...], b_ref[...])


def _ffn(alpha, x, w1, w2, g, b, tm):
    n = x.shape[0]
    row = pl.BlockSpec((tm, D_MODEL), lambda i: (i, 0))
    return pl.pallas_call(
        functools.partial(_ffn_kernel, alpha), grid=(n // tm,),
        in_specs=[row, _const_spec((D_MODEL, D_FF)), _const_spec((D_FF, D_MODEL)),
                  _const_spec((1, D_MODEL)), _const_spec((1, D_MODEL))],
        out_specs=row, out_shape=jax.ShapeDtypeStruct((n, D_MODEL), F32),
        compiler_params=_cparams(("parallel",), _mib(56)), name="relu2_mlp",
    )(x, w1, w2, g, b)


def _sample_s5_sg_kernel(u_ref, zc_ref, h0_ref, bblk_ref, cblk_ref, d_ref, a_ref, g_ref, b_ref, wd_ref, bd_ref,
                         y_ref, h_ref, sg_ref, vsg_ref):
    u = u_ref[...]
    a = a_ref[...]
    a_re, a_im = a[:, :S5_FLAT], a[:, S5_FLAT:]
    h0 = h0_ref[...]
    h0_re, h0_im = h0[:, :S5_FLAT], h0[:, S5_FLAT:]
    bu = _dot(u, bblk_ref[...])
    h_re = bu[:, :S5_FLAT] + (a_re * h0_re - a_im * h0_im)
    h_im = bu[:, S5_FLAT:] + (a_re * h0_im + a_im * h0_re)
    h_ref[:, :S5_FLAT] = h_re
    h_ref[:, S5_FLAT:] = h_im
    y_ref[...] = _dot(h_ref[...], cblk_ref[...]) + d_ref[...] * u
    z = jax.nn.gelu(zc_ref[...])
    v = _layer_norm(z[:, SG_WIDTH:], g_ref[...], b_ref[...])
    vsg_ref[...] = v
    sg_ref[...] = z[:, :SG_WIDTH] * (wd_ref[...] * v + bd_ref[...])


def _sample_s5_sg(u, zc, h0, bblk, cblk, dvec, avec, g, b, wd, bd):
    n = u.shape[0]
    full = lambda a: _const_spec(a.shape)
    args = (u, zc, h0, bblk, cblk, dvec, avec, g, b, wd, bd)
    return pl.pallas_call(
        _sample_s5_sg_kernel, grid=(1,), in_specs=[full(a) for a in args],
        out_specs=[_const_spec((n, S5_WIDTH)), _const_spec((n, 2 * S5_FLAT)), _const_spec((n, SG_WIDTH)),
                   _const_spec((n, SG_WIDTH))],
        out_shape=[jax.ShapeDtypeStruct((n, S5_WIDTH), F32), jax.ShapeDtypeStruct((n, 2 * S5_FLAT), F32),
                   jax.ShapeDtypeStruct((n, SG_WIDTH), F32), jax.ShapeDtypeStruct((n, SG_WIDTH), F32)],
        compiler_params=_cparams(("arbitrary",), _mib(24)), name="sample_s5_sgate",
    )(*args)


def _sample_scores_kernel(n_pages, pt_ref, iq_ref, iw_ref, *refs):
    page_refs, o_ref = refs[:n_pages], refs[n_pages]
    iq = iq_ref[0]
    iw = iw_ref[0] * (IDX_HEADS ** -0.5)
    for j in range(n_pages):
        dots = _dot_nt(iq, page_refs[j][0, 0]) * (IDX_DIM ** -0.5)
        o_ref[0, :, j * PAGE_SIZE:(j + 1) * PAGE_SIZE] = jnp.sum(iw * jnp.maximum(dots, 0.0), axis=0, keepdims=True)


def _sample_scores(page_table, iq, iw, cache_kidx, layer):
    n, n_pages = page_table.shape
    page = lambda j: pl.BlockSpec((1, 1, PAGE_SIZE, IDX_DIM), lambda i, pt: (layer, pt[i, j], 0, 0))
    grid_spec = pltpu.PrefetchScalarGridSpec(
        num_scalar_prefetch=1, grid=(n,),
        in_specs=[pl.BlockSpec((1, SUBLANES, IDX_DIM), lambda i, pt: (i, 0, 0)),
                  pl.BlockSpec((1, SUBLANES, 1), lambda i, pt: (i, 0, 0))] + [page(j) for j in range(n_pages)],
        out_specs=pl.BlockSpec((1, 1, n_pages * PAGE_SIZE), lambda i, pt: (i, 0, 0)))
    return pl.pallas_call(
        functools.partial(_sample_scores_kernel, n_pages), grid_spec=grid_spec,
        out_shape=jax.ShapeDtypeStruct((n, 1, n_pages * PAGE_SIZE), F32),
        compiler_params=_cparams(("arbitrary",), _mib(24)), name="sample_idx_scores",
    )(page_table, iq, iw, *([cache_kidx] * n_pages))


def _sample_select_kernel(k_sel, sc_ref, iq_ref, ikw_ref, o_ref, key_scr):
    n, past = sc_ref.shape
    iq = iq_ref[...]
    ikw = ikw_ref[...]
    ik = ikw[:, :IDX_DIM]
    own = jnp.zeros((n, 1), F32)
    for h in range(IDX_HEADS):
        dots = jnp.sum(iq[:, h * IDX_DIM:(h + 1) * IDX_DIM] * ik, axis=1, keepdims=True) * (IDX_DIM ** -0.5)
        own = own + ikw[:, IDX_DIM + h:IDX_DIM + h + 1] * (IDX_HEADS ** -0.5) * jnp.maximum(dots, 0.0)
    key_scr[...] = _score_keys(sc_ref[...])
    xkey = _score_keys(own)
    thr, jstar = _topk_threshold(key_scr, past, k_sel, xkey)
    lane = lax.broadcasted_iota(I32, (n, LANES), 1)
    for c in range(past // LANES):
        sel = _selected(key_scr[:, c * LANES:(c + 1) * LANES], lane + c * LANES, thr, jstar)
        o_ref[:, c * LANES:(c + 1) * LANES] = jnp.where(sel, 1.0, 0.0)
    own_sel = _selected(xkey, jnp.full((n, 1), past, I32), thr, jstar)
    o_ref[:, past:] = jnp.broadcast_to(jnp.where(own_sel, 1.0, 0.0), (n, LANES))


def _sample_select(scores, iq, ikw, k_sel):
    n, past = scores.shape
    return pl.pallas_call(
        functools.partial(_sample_select_kernel, k_sel), grid=(1,),
        in_specs=[_const_spec(scores.shape), _const_spec(iq.shape), _const_spec(ikw.shape)],
        out_specs=_const_spec((n, past + LANES)), out_shape=jax.ShapeDtypeStruct((n, past + LANES), F32),
        scratch_shapes=[pltpu.VMEM((n, past), I32)],
        compiler_params=_cparams(("arbitrary",), _mib(24)), name="sample_topk_mask",
    )(scores, iq, ikw)


def _sample_attend_kernel(n_pages, pt_ref, q_ref, kn_ref, vn_ref, m_ref, *refs):
    k_refs, v_refs, o_ref = refs[:n_pages], refs[n_pages:2 * n_pages], refs[2 * n_pages]
    rows = PAGE_SIZE * A_HEADS
    q8 = q_ref[0]
    qt = jnp.broadcast_to(q8[None], (PAGE_SIZE, A_HEADS, A_HEAD_DIM)).reshape(rows, A_HEAD_DIM)
    expand = jnp.where(jnp.right_shift(lax.broadcasted_iota(I32, (rows, PAGE_SIZE), 0), A_HEADS.bit_length() - 1)
                       == lax.broadcasted_iota(I32, (rows, PAGE_SIZE), 1), 1.0, 0.0).astype(BF16)
    scale = A_HEAD_DIM ** -0.5

    def update(state, s, v, lead):
        m_old, l_old, acc = state
        m_new = jnp.maximum(m_old, jnp.max(s, axis=0))
        m_use = jnp.where(m_new == -jnp.inf, 0.0, m_new)
        p = jnp.exp(s - m_use[None])
        resc = jnp.exp(m_old - m_use)
        return (m_new, resc * l_old + jnp.sum(p, axis=0), resc * acc + jnp.sum(p * v, axis=0))

    state = (jnp.full((A_HEADS, 1), -jnp.inf, F32), jnp.zeros((A_HEADS, 1), F32), jnp.zeros((A_HEADS, A_HEAD_DIM), F32))
    for j in range(n_pages):
        k2 = k_refs[j][0, 0].reshape(rows, A_HEAD_DIM)
        mrow = jnp.broadcast_to(m_ref[0, :, j * PAGE_SIZE:(j + 1) * PAGE_SIZE], (SUBLANES, PAGE_SIZE))
        mcol = _dot_nt(expand, mrow)[:, 0:1]
        s = jnp.sum(k2 * qt, axis=1, keepdims=True) * scale
        s = jnp.where(mcol > 0.5, s, -jnp.inf)
        state = update(state, s.reshape(PAGE_SIZE, A_HEADS, 1), v_refs[j][0, 0], PAGE_SIZE)
    s_own = jnp.sum(kn_ref[0] * q8, axis=1, keepdims=True) * scale
    own_sel = m_ref[0, :, n_pages * PAGE_SIZE:n_pages * PAGE_SIZE + 1]
    s_own = jnp.where(own_sel > 0.5, s_own, -jnp.inf)
    _, l, acc = update(state, s_own[None], vn_ref[0][None], 1)
    o_ref[0] = acc / l


def _sample_attend(page_table, q, k_new, v_new, mask, cache_k, cache_v, layer):
    n, n_pages = page_table.shape
    page = lambda j: pl.BlockSpec((1, 1, PAGE_SIZE, A_HEADS, A_HEAD_DIM), lambda i, pt: (layer, pt[i, j], 0, 0, 0))
    tok = pl.BlockSpec((1, A_HEADS, A_HEAD_DIM), lambda i, pt: (i, 0, 0))
    grid_spec = pltpu.PrefetchScalarGridSpec(
        num_scalar_prefetch=1, grid=(n,),
        in_specs=[tok, tok, tok, pl.BlockSpec((1, 1, mask.shape[-1]), lambda i, pt: (i, 0, 0))]
        + [page(j) for j in range(n_pages)] * 2,
        out_specs=tok)
    return pl.pallas_call(
        functools.partial(_sample_attend_kernel, n_pages), grid_spec=grid_spec,
        out_shape=jax.ShapeDtypeStruct((n, A_HEADS, A_HEAD_DIM), F32),
        compiler_params=_cparams(("arbitrary",), _mib(56)), name="sample_dsa_attend",
    )(page_table, q, k_new, v_new, mask, *([cache_k] * n_pages), *([cache_v] * n_pages))


def _linear_kernel(x_ref, w_ref, o_ref):
    o_ref[...] = _dot(x_ref[...], w_ref[...])


def _linear(x, w):
    n, m = x.shape[0], w.shape[1]
    return pl.pallas_call(
        _linear_kernel, grid=(1,), in_specs=[_const_spec(x.shape), _const_spec(w.shape)],
        out_specs=_const_spec((n, m)), out_shape=jax.ShapeDtypeStruct((n, m), F32),
        compiler_params=_cparams(("arbitrary",), _mib(24)), name="linear",
    )(x, w)


def _linear_res_ln_kernel(alpha, o_ref, w_ref, x_ref, g_ref, b_ref, y_ref):
    y_ref[...] = _layer_norm(alpha * x_ref[...] + _dot(o_ref[...], w_ref[...]), g_ref[...], b_ref[...])


def _linear_res_ln(alpha, o, w, x, g, b):
    args = (o, w, x, g, b)
    return pl.pallas_call(
        functools.partial(_linear_res_ln_kernel, alpha), grid=(1,), in_specs=[_const_spec(a.shape) for a in args],
        out_specs=_const_spec(x.shape), out_shape=jax.ShapeDtypeStruct(x.shape, F32),
        compiler_params=_cparams(("arbitrary",), _mib(24)), name="linear_residual_norm",
    )(*args)


def _mem_sample_kernel(bs, q_ref, mk_ref, mv_ref, o_ref):
    for i in range(bs):
        q = q_ref[i]
        s = jnp.sum(mk_ref[0, i] * q[None], axis=-1, keepdims=True) * (MEM_HEAD_DIM ** -0.5)
        m = jnp.max(s, axis=0, keepdims=True)
        p = jnp.exp(s - m)
        l = jnp.sum(p, axis=0)
        o_ref[i] = jnp.sum(p * mv_ref[0, i], axis=0) / l


def _mem_sample(q, cache_mk, cache_mv, layer, bs):
    n = q.shape[0]
    cache = pl.BlockSpec((1, bs, MEM_TOKENS, MEM_HEADS, MEM_HEAD_DIM), lambda i: (layer, i, 0, 0, 0))
    tok = pl.BlockSpec((bs, MEM_HEADS, MEM_HEAD_DIM), lambda i: (i, 0, 0))
    return pl.pallas_call(
        functools.partial(_mem_sample_kernel, bs), grid=(n // bs,),
        in_specs=[tok, cache, cache], out_specs=tok,
        out_shape=jax.ShapeDtypeStruct((n, MEM_HEADS, MEM_HEAD_DIM), F32),
        compiler_params=_cparams(("parallel",), _mib(40)), name="mem_attend_sample",
    )(q, cache_mk, cache_mv)


def _pack_w_in(w_in):
    lead = w_in.shape[:-1]
    pts = np.cumsum([A_WIDTH, A_WIDTH, A_WIDTH, IDX_HEADS * IDX_DIM, IDX_DIM, IDX_HEADS, S5_WIDTH]).tolist()
    q, k, v, iq, ik, iw, u5, zc = jnp.split(w_in, pts, axis=-1)
    pad = jnp.zeros(lead + (IKW_PAD,), w_in.dtype)
    return jnp.concatenate([q, k, v, iq, ik, iw, pad, u5, zc], axis=-1).astype(BF16)


def _rope_tables(pos):
    half = A_HEAD_DIM // 2
    inv_freq = ROPE_THETA ** (-jnp.arange(half, dtype=F32) / half)
    ang = pos.astype(F32)[:, None] * inv_freq[None, :]
    c, s = jnp.cos(ang), jnp.sin(ang)
    return jnp.tile(c, (1, LANES // half)), jnp.concatenate([-s, s] * (LANES // A_HEAD_DIM), axis=1)


def _s5_matrices(lam_re, lam_im, log_dt, b_re, b_im, c_re, c_im, d):
    dt = jnp.exp(log_dt)[:, None]
    mag = jnp.exp(dt * lam_re)
    ang = dt * lam_im
    a_re = mag * jnp.cos(ang)
    a_im = mag * jnp.sin(ang)
    den = jnp.square(lam_re) + jnp.square(lam_im)
    z_re = ((a_re - 1.0) * lam_re + a_im * lam_im) / den
    z_im = (a_im * lam_re - (a_re - 1.0) * lam_im) / den
    bb_re = z_re[..., None] * b_re - z_im[..., None] * b_im
    bb_im = z_re[..., None] * b_im + z_im[..., None] * b_re
    eye = jnp.eye(S5_GROUPS, dtype=F32)
    in_re = jnp.einsum("gph,gk->ghkp", bb_re, eye).reshape(S5_WIDTH, S5_FLAT)
    in_im = jnp.einsum("gph,gk->ghkp", bb_im, eye).reshape(S5_WIDTH, S5_FLAT)
    bblk = jnp.concatenate([in_re, in_im], axis=1).astype(BF16)
    out_re = jnp.einsum("ghp,gk->gpkh", c_re, eye).reshape(S5_FLAT, S5_WIDTH)
    out_im = jnp.einsum("ghp,gk->gpkh", -c_im, eye).reshape(S5_FLAT, S5_WIDTH)
    cblk = jnp.concatenate([out_re, out_im], axis=0).astype(BF16)
    avec = jnp.concatenate([a_re.reshape(1, S5_FLAT), a_im.reshape(1, S5_FLAT)], axis=1)
    return bblk, cblk, d.reshape(1, S5_WIDTH), avec


def _pick_tile(n, pref):
    t = min(n, pref)
    assert n % t == 0
    return t


def kernel(x_prompt, x_sample, cache_k, cache_v, cache_kidx, cache_mem_k, cache_mem_v, state_s5_re, state_s5_im,
           page_table, mem_prompt, w_in, w_a_out, s5_lambda_re, s5_lambda_im, s5_log_dt, s5_b_re, s5_b_im,
           s5_c_re, s5_c_im, s5_d, w_glu_a, w_glu_b, sg_ln_g, sg_ln_b, sg_w, sg_b, w_c_out, w_gate, b_gate,
           w_mix_out, ln1_g, ln1_b, w_mq, w_mk, w_mv, w_mo, ln2_g, ln2_b, w_ff1, w_ff2, ln3_g, ln3_b):
    depth = w_in.shape[0]
    bp, tp, _ = x_prompt.shape
    db, ds, _ = x_sample.shape
    assert ds == 1 and tp % S5_CHUNK == 0 and tp % SG_CHUNK == 0
    n_pages = page_table.shape[1]
    past = n_pages * PAGE_SIZE
    alpha = (2 * depth) ** 0.25
    bf = lambda a: a.astype(BF16)
    row = lambda a, l: a[l][None, :]

    w_in_p = _pack_w_in(w_in)
    w_a_out_b, w_glu_a_b, w_glu_b_b, w_c_out_b = bf(w_a_out), bf(w_glu_a), bf(w_glu_b), bf(w_c_out)
    w_gate_b, w_mix_b = bf(w_gate), bf(w_mix_out)
    w_mq_b, w_mk_b, w_mv_b, w_mo_b = bf(w_mq), bf(w_mk), bf(w_mv), bf(w_mo)
    w_ff1_b, w_ff2_b = bf(w_ff1), bf(w_ff2)
    s5m = [_s5_matrices(s5_lambda_re[l], s5_lambda_im[l], s5_log_dt[l], s5_b_re[l], s5_b_im[l], s5_c_re[l],
                        s5_c_im[l], s5_d[l]) for l in range(depth)]

    n = bp * tp
    tm = _pick_tile(tp, 256)
    cos_p, sin_p = _rope_tables(jnp.arange(tp))
    k_sel_p = min(TOPK_MAX, tp // 4)
    qb = _pick_tile(tp, 128)
    key_step = _pick_tile(tp, 512)
    sg_bias = [jnp.repeat(sg_b[l].T, SG_GROUP_CH, axis=1) for l in range(depth)]
    mem2d = mem_prompt.reshape(bp * MEM_TOKENS, D_MODEL)
    h0 = jnp.zeros((bp, 1, 2 * S5_FLAT), F32)

    h = x_prompt.reshape(n, D_MODEL)
    kp, vp, ikp, mkp, mvp, s5p = [], [], [], [], [], []
    for l in range(depth):
        bblk, cblk, dvec, avec = s5m[l]
        q, k, v, iq, ikw, u5, zc, kbf, vbf = _proj(h, w_in_p[l], cos_p, sin_p, tm)
        r3 = lambda a: a.reshape(bp, tp, a.shape[-1])
        att = _dsa_prompt(r3(q), r3(iq), r3(ikw), r3(kbf), r3(vbf), k_sel_p, qb, key_step)
        u5p = u5.reshape(bp, tp // S5_CHUNK, SUBLANES, S5_SEG, S5_WIDTH).swapaxes(2, 3).reshape(bp, tp, S5_WIDTH)
        y5p, hfin = _s5_prompt(u5p, bblk, cblk, dvec, avec, h0)
        y5 = y5p.reshape(bp, tp // S5_CHUNK, S5_SEG, SUBLANES, S5_WIDTH).swapaxes(2, 3)
        sg = _sgate_prompt(zc, row(sg_ln_g, l), row(sg_ln_b, l), sg_w[l], sg_bias[l], _pick_tile(tp, 512))
        h = _merge(alpha, h, att.reshape(n, A_WIDTH), y5.reshape(n, S5_WIDTH), sg, w_a_out_b[l], w_glu_a_b[l],
                   w_glu_b_b[l], w_c_out_b[l], w_gate_b[l], row(b_gate, l), w_mix_b[l], row(ln1_g, l), row(ln1_b, l), tm)
        mk, mv = _two_linear(mem2d, w_mk_b[l], w_mv_b[l], _pick_tile(bp * MEM_TOKENS, 256))
        h = _mem_prompt(alpha, h.reshape(bp, tp, D_MODEL), mk.reshape(bp, MEM_TOKENS, MEM_WIDTH),
                        mv.reshape(bp, MEM_TOKENS, MEM_WIDTH), w_mq_b[l], w_mo_b[l], row(ln2_g, l), row(ln2_b, l),
                        tm).reshape(n, D_MODEL)
        h = _ffn(alpha, h, w_ff1_b[l], w_ff2_b[l], row(ln3_g, l), row(ln3_b, l), tm)
        kp.append(k.reshape(bp, tp, A_HEADS, A_HEAD_DIM))
        vp.append(v.reshape(bp, tp, A_HEADS, A_HEAD_DIM))
        ikp.append(ikw[:, :IDX_DIM].reshape(bp, tp, IDX_DIM))
        mkp.append(mk.reshape(bp, MEM_TOKENS, MEM_HEADS, MEM_HEAD_DIM))
        mvp.append(mv.reshape(bp, MEM_TOKENS, MEM_HEADS, MEM_HEAD_DIM))
        s5p.append(hfin.reshape(bp, 2, S5_GROUPS, S5_STATE))
    y_prompt = h.reshape(bp, tp, D_MODEL)

    cos_s, sin_s = _rope_tables(jnp.full((db,), past, jnp.int32))
    k_sel_s = min(TOPK_MAX, (past + ds) // 4)
    h = x_sample.reshape(db, D_MODEL)
    ks, vs, iks, s5s, sgs = [], [], [], [], []
    for l in range(depth):
        bblk, cblk, dvec, avec = s5m[l]
        q, k, v, iq, ikw, u5, zc, _, _ = _proj(h, w_in_p[l], cos_s, sin_s, db)
        head_pad = ((0, 0), (0, SUBLANES - IDX_HEADS), (0, 0))
        scores = _sample_scores(page_table, jnp.pad(iq.reshape(db, IDX_HEADS, IDX_DIM), head_pad),
                                jnp.pad(ikw[:, IDX_DIM:IDX_DIM + IDX_HEADS].reshape(db, IDX_HEADS, 1), head_pad),
                                cache_kidx, l)
        mask = _sample_select(scores.reshape(db, past), iq, ikw, k_sel_s)
        hd = lambda a: a.reshape(db, A_HEADS, A_HEAD_DIM)
        att = _sample_attend(page_table, hd(q), hd(k), hd(v), mask.reshape(db, 1, past + LANES), cache_k, cache_v, l)
        h0s = jnp.concatenate([state_s5_re[l].reshape(db, S5_FLAT), state_s5_im[l].reshape(db, S5_FLAT)], axis=1)
        wd = jnp.repeat(sg_w[l][:, 0, 0], SG_GROUP_CH)[None, :]
        bd = jnp.repeat(sg_b[l][:, 0], SG_GROUP_CH)[None, :]
        y5, hnew, sg, vsg = _sample_s5_sg(u5, zc, h0s, bblk, cblk, dvec, avec, row(sg_ln_g, l), row(sg_ln_b, l), wd, bd)
        h = _merge(alpha, h, att.reshape(db, A_WIDTH), y5, sg, w_a_out_b[l], w_glu_a_b[l], w_glu_b_b[l], w_c_out_b[l],
                   w_gate_b[l], row(b_gate, l), w_mix_b[l], row(ln1_g, l), row(ln1_b, l), db)
        mq = _linear(h, w_mq_b[l])
        mo = _mem_sample(mq.reshape(db, MEM_HEADS, MEM_HEAD_DIM), cache_mem_k, cache_mem_v, l, 4)
        h = _linear_res_ln(alpha, mo.reshape(db, MEM_WIDTH), w_mo_b[l], h, row(ln2_g, l), row(ln2_b, l))
        h = _ffn(alpha, h, w_ff1_b[l], w_ff2_b[l], row(ln3_g, l), row(ln3_b, l), db)
        ks.append(k.reshape(db, ds, A_HEADS, A_HEAD_DIM))
        vs.append(v.reshape(db, ds, A_HEADS, A_HEAD_DIM))
        iks.append(ikw[:, :IDX_DIM].reshape(db, ds, IDX_DIM))
        s5s.append(hnew.reshape(db, 2, S5_GROUPS, S5_STATE))
        sgs.append(vsg.reshape(db, ds, SG_WIDTH))
    y_sample = h.reshape(db, ds, D_MODEL)

    s5p, s5s = jnp.stack(s5p), jnp.stack(s5s)
    return (y_prompt, y_sample, jnp.stack(kp), jnp.stack(vp), jnp.stack(ikp), jnp.stack(mkp), jnp.stack(mvp),
            s5p[:, :, 0], s5p[:, :, 1], jnp.stack(ks), jnp.stack(vs), jnp.stack(iks), s5s[:, :, 0], s5s[:, :, 1],
            jnp.stack(sgs))
```

```python
import functools
import math

import numpy as np
import jax
import jax.numpy as jnp
from jax import lax
from jax.experimental import pallas as pl
from jax.experimental.pallas import tpu as pltpu

F32 = jnp.float32
BF16 = jnp.bfloat16
I32 = jnp.int32

D_MODEL = 1024
A_HEADS = 8
A_HEAD_DIM = 64
A_WIDTH = A_HEADS * A_HEAD_DIM
IDX_HEADS = 4
IDX_DIM = 64
TOPK_MAX = 256
ROPE_THETA = 10000.0
S5_GROUP_CH = 16
S5_GROUPS = 16
S5_WIDTH = S5_GROUPS * S5_GROUP_CH
S5_STATE = 64
S5_FLAT = S5_GROUPS * S5_STATE
SG_CHUNK = 128
SG_GROUPS = 4
SG_WIDTH = 256
SG_GROUP_CH = SG_WIDTH // SG_GROUPS
MEM_TOKENS = 256
MEM_HEADS = 4
MEM_HEAD_DIM = 128
MEM_WIDTH = MEM_HEADS * MEM_HEAD_DIM
D_FF = 4 * D_MODEL
N_BRANCH = 3
LN_EPS = 1e-5
PAGE_SIZE = 128

LANES = 128
SUBLANES = 8
VMEM_PHYSICAL_BYTES = 64 * 1024 * 1024

IKW_PAD = LANES - IDX_DIM - IDX_HEADS
OFF_Q = 0
OFF_K = OFF_Q + A_WIDTH
OFF_V = OFF_K + A_WIDTH
OFF_IQ = OFF_V + A_WIDTH
OFF_IKW = OFF_IQ + IDX_HEADS * IDX_DIM
OFF_U5 = OFF_IKW + LANES
OFF_ZC = OFF_U5 + S5_WIDTH
IN_PACKED = OFF_ZC + 2 * SG_WIDTH

INT_MIN = -2 ** 31
NEG_INF_KEY = int(np.array([0xFF800000 ^ 0x7FFFFFFF], dtype=np.uint32).view(np.int32)[0])


def _cparams(semantics, vmem_bytes):
    assert vmem_bytes < VMEM_PHYSICAL_BYTES
    return pltpu.CompilerParams(dimension_semantics=semantics, vmem_limit_bytes=int(vmem_bytes))


def _mib(n):
    return n * 1024 * 1024


def _const_spec(shape):
    n = len(shape)
    return pl.BlockSpec(shape, lambda *_: (0,) * n)


def _layer_norm(x, g, b):
    mu = jnp.mean(x, axis=-1, keepdims=True)
    xc = x - mu
    var = jnp.mean(xc * xc, axis=-1, keepdims=True)
    return xc * lax.rsqrt(var + LN_EPS) * g + b


def _dot(a, b):
    return jnp.dot(a.astype(BF16), b.astype(BF16), preferred_element_type=F32)


def _dot_nt(a, b):
    return lax.dot_general(a.astype(BF16), b.astype(BF16), (((1,), (1,)), ((), ())),
                           preferred_element_type=F32)


def _proj_kernel(emit_rows, x_ref, w_ref, cos_ref, sin_ref, q_ref, iq_ref, u5_ref, zc_ref, kbf_ref, ikbf_ref,
                 kt_ref, vt_ref, vtbf_ref, ikwt_ref, *row_refs):
    y = _dot(x_ref[...], w_ref[...])
    cos = cos_ref[...]
    sin = sin_ref[...]
    lane = lax.broadcasted_iota(I32, cos.shape, 1)
    first_half = (lane & (A_HEAD_DIM // 2)) == 0

    def rope(t, c, s):
        partner = jnp.where(first_half, pltpu.roll(t, LANES - A_HEAD_DIM // 2, 1), pltpu.roll(t, A_HEAD_DIM // 2, 1))
        return t * c + partner * s

    for g in range(A_WIDTH // LANES):
        sl = slice(g * LANES, (g + 1) * LANES)
        q_ref[:, sl] = rope(y[:, OFF_Q + g * LANES:OFF_Q + (g + 1) * LANES], cos, sin)
        kg = rope(y[:, OFF_K + g * LANES:OFF_K + (g + 1) * LANES], cos, sin)
        kbf_ref[:, sl] = kg.astype(BF16)
        kt_ref[0, sl, :] = kg.T
        vg = y[:, OFF_V + g * LANES:OFF_V + (g + 1) * LANES]
        vt = vg.T
        vt_ref[0, sl, :] = vt
        vtbf_ref[0, sl, :] = vt.astype(BF16)
        if emit_rows:
            row_refs[0][:, sl] = kg
            row_refs[1][:, sl] = vg
    for g in range(IDX_HEADS * IDX_DIM // LANES):
        iq_ref[:, g * LANES:(g + 1) * LANES] = rope(y[:, OFF_IQ + g * LANES:OFF_IQ + (g + 1) * LANES], cos, sin)
    is_key = lane < IDX_DIM
    ikw = rope(y[:, OFF_IKW:OFF_IKW + LANES], jnp.where(is_key, cos, 1.0), jnp.where(is_key, sin, 0.0))
    ikbf_ref[...] = ikw[:, :IDX_DIM].astype(BF16)
    ikwt_ref[0] = ikw.T
    if emit_rows:
        row_refs[2][...] = ikw
    u5_ref[...] = y[:, OFF_U5:OFF_U5 + S5_WIDTH]
    zc_ref[...] = y[:, OFF_ZC:OFF_ZC + 2 * SG_WIDTH]


def _proj(x, w, cos, sin, seq, tm, emit_rows):
    n = x.shape[0]
    nt = seq // tm
    row = lambda w_: pl.BlockSpec((tm, w_), lambda i: (i, 0))
    tab = pl.BlockSpec((tm, LANES), lambda i: (i % nt, 0))
    tr = lambda w_: pl.BlockSpec((1, w_, tm), lambda i: (i // nt, 0, i % nt))
    sds = jax.ShapeDtypeStruct
    row_out = [(A_WIDTH, F32), (IDX_HEADS * IDX_DIM, F32), (S5_WIDTH, F32), (2 * SG_WIDTH, F32), (A_WIDTH, BF16),
               (IDX_DIM, BF16)]
    tr_out = [(A_WIDTH, F32), (A_WIDTH, F32), (A_WIDTH, BF16), (LANES, F32)]
    extra = [(A_WIDTH, F32), (A_WIDTH, F32), (LANES, F32)] if emit_rows else []
    out_shape = ([sds((n, w_), dt) for w_, dt in row_out] + [sds((n // seq, w_, seq), dt) for w_, dt in tr_out]
                 + [sds((n, w_), dt) for w_, dt in extra])
    out_specs = [row(w_) for w_, _ in row_out] + [tr(w_) for w_, _ in tr_out] + [row(w_) for w_, _ in extra]
    return pl.pallas_call(
        functools.partial(_proj_kernel, emit_rows), grid=(n // tm,),
        in_specs=[row(D_MODEL), _const_spec((D_MODEL, IN_PACKED)), tab, tab],
        out_specs=out_specs, out_shape=out_shape,
        compiler_params=_cparams(("parallel",), _mib(40)), name="in_proj_rope",
    )(x, w, cos, sin)


KEY_CHUNK = 512


def _score_keys(score):
    score = jnp.where(score == 0.0, 0.0, score)
    bits = lax.bitcast_convert_type(score, I32)
    return jnp.where(bits < 0, bits ^ 0x7FFFFFFF, bits)


def _rows_reduce(v, op):
    parts = [v[i * SUBLANES:(i + 1) * SUBLANES] for i in range(v.shape[0] // SUBLANES)]
    while len(parts) > 1:
        parts = [op(parts[i], parts[i + 1]) for i in range(0, len(parts), 2)]
    return parts[0]


def _chunk_rows(c):
    return pl.ds(pl.multiple_of(c * KEY_CHUNK, KEY_CHUNK), KEY_CHUNK)


def _count(key_ref, n_chunks, pred, extra=None):
    r = key_ref.shape[1]

    def body(c, acc):
        return acc + _rows_reduce(pred(key_ref[_chunk_rows(c), :], c * KEY_CHUNK), jnp.add)

    acc = lax.fori_loop(0, n_chunks, body, jnp.zeros((SUBLANES, r), I32))
    cnt = jnp.sum(acc, axis=0, keepdims=True)
    return cnt if extra is None else cnt + extra


def _as_int(mask):
    return jnp.where(mask, 1, 0)


def _topk_threshold(key_ref, n_chunks, k_sel, last_row, xkey=None):
    r = key_ref.shape[1]

    def bit_body(i, prefix):
        cand = prefix | jnp.left_shift(jnp.int32(1), 31 - i)
        cand_s = cand ^ INT_MIN
        cnt = _count(key_ref, n_chunks, lambda ch, _: _as_int(ch >= cand_s),
                     None if xkey is None else _as_int(xkey >= cand_s))
        return jnp.where(cnt >= k_sel, cand, prefix)

    prefix = lax.fori_loop(0, 32, bit_body, jnp.zeros((1, r), I32))
    thr = prefix ^ INT_MIN
    cnt_gt = _count(key_ref, n_chunks, lambda ch, _: _as_int(ch > thr), None if xkey is None else _as_int(xkey > thr))
    cnt_eq = _count(key_ref, n_chunks, lambda ch, _: _as_int(ch == thr), None if xkey is None else _as_int(xkey == thr))
    need = k_sel - cnt_gt
    surplus = jnp.where(cnt_eq > need, _as_int(thr > NEG_INF_KEY), 0)
    any_surplus = jnp.max(surplus) > 0
    row = lax.broadcasted_iota(I32, (KEY_CHUNK, r), 0)
    n_bits = int(last_row).bit_length()

    def tie_search():
        def body(i, lo):
            cand = lo + jnp.left_shift(jnp.int32(1), n_bits - 1 - i)
            extra = None if xkey is None else jnp.where(xkey == thr, _as_int(cand >= last_row), 0)
            cnt = _count(key_ref, n_chunks, lambda ch, r0: jnp.where(ch == thr, _as_int(row + r0 <= cand), 0), extra)
            return jnp.where(cnt < need, cand, lo)

        return lax.fori_loop(0, n_bits, body, jnp.full((1, r), -1, I32)) + 1

    jstar = lax.cond(any_surplus, tie_search, lambda: jnp.full((1, r), last_row, I32))
    return thr, jstar


def _selected(ch, row, thr, jstar):
    return jnp.where(ch == thr, _as_int(row <= jstar), _as_int(ch > thr)) > 0


def _dsa_prompt_kernel(k_sel, q_ref, iq_ref, iwt_ref, ikbf_ref, kbf_ref, vtbf_ref, o_ref, key_scr, bias_scr,
                       qpair_scr, m_scr, l_scr, acc_scr):
    qb = q_ref.shape[1]
    t = kbf_ref.shape[1]
    q0 = pl.program_id(1) * qb
    n_chunks = (q0 + qb + KEY_CHUNK - 1) // KEY_CHUNK
    qcol = q0 + lax.broadcasted_iota(I32, (1, qb), 1)
    row = lax.broadcasted_iota(I32, (KEY_CHUNK, qb), 0)
    iwt = iwt_ref[0][:IDX_HEADS] * (IDX_HEADS ** -0.5)
    iq = iq_ref[0]
    iqh = [iq[:, h * IDX_DIM:(h + 1) * IDX_DIM].astype(BF16) for h in range(IDX_HEADS)]

    def score_chunk(c, carry):
        ik = ikbf_ref[0, _chunk_rows(c), :]
        score = jnp.zeros((KEY_CHUNK, qb), F32)
        for h in range(IDX_HEADS):
            dots = _dot_nt(ik, iqh[h]) * (IDX_DIM ** -0.5)
            score = score + iwt[h:h + 1, :] * jnp.maximum(dots, 0.0)
        key_scr[_chunk_rows(c), :] = _score_keys(jnp.where(row + c * KEY_CHUNK <= qcol, score, -jnp.inf))
        return carry

    lax.fori_loop(0, n_chunks, score_chunk, 0)
    thr, jstar = _topk_threshold(key_scr, n_chunks, k_sel, t - 1)

    def bias_chunk(c, carry):
        krow = row + c * KEY_CHUNK
        sel = _selected(key_scr[_chunk_rows(c), :], krow, thr, jstar)
        bias_scr[_chunk_rows(c), :] = jnp.where(krow <= qcol, jnp.where(sel, 0.0, -jnp.inf), -jnp.inf)
        return carry

    lax.fori_loop(0, n_chunks, bias_chunk, 0)

    qs = q_ref[0] * (A_HEAD_DIM ** -0.5)
    lane = lax.broadcasted_iota(I32, (qb, LANES), 1)
    for g in range(A_WIDTH // LANES):
        qg = qs[:, g * LANES:(g + 1) * LANES]
        qpair_scr[g, :qb, :] = jnp.where(lane < A_HEAD_DIM, qg, 0.0).astype(BF16)
        qpair_scr[g, qb:, :] = jnp.where(lane < A_HEAD_DIM, 0.0, qg).astype(BF16)
    m_scr[...] = jnp.full(m_scr.shape, -jnp.inf, F32)
    l_scr[...] = jnp.zeros(l_scr.shape, F32)
    acc_scr[...] = jnp.zeros(acc_scr.shape, F32)

    def attend_chunk(c, carry):
        rows = _chunk_rows(c)
        bias = bias_scr[rows, :]
        for g in range(A_WIDTH // LANES):
            s_pair = _dot_nt(kbf_ref[0, rows, g * LANES:(g + 1) * LANES], qpair_scr[g])
            for half in range(2):
                h = 2 * g + half
                s = s_pair[:, half * qb:(half + 1) * qb] + bias
                m_old = m_scr[h:h + 1, :]
                m_new = jnp.maximum(m_old, jnp.max(_rows_reduce(s, jnp.maximum), axis=0, keepdims=True))
                m_use = jnp.where(m_new == -jnp.inf, 0.0, m_new)
                p = jnp.exp(s - m_use)
                resc = jnp.exp(m_old - m_use)
                m_scr[h:h + 1, :] = m_new
                l_scr[h:h + 1, :] = resc * l_scr[h:h + 1, :] + jnp.sum(_rows_reduce(p, jnp.add), axis=0, keepdims=True)
                hd = slice(h * A_HEAD_DIM, (h + 1) * A_HEAD_DIM)
                acc_scr[hd, :] = resc * acc_scr[hd, :] + _dot(vtbf_ref[0, hd, rows], p)
        return carry

    lax.fori_loop(0, n_chunks, attend_chunk, 0)
    out_row = lax.broadcasted_iota(I32, (LANES, qb), 0)
    for g in range(A_WIDTH // LANES):
        l_pair = jnp.where(out_row < A_HEAD_DIM, l_scr[2 * g:2 * g + 1, :], l_scr[2 * g + 1:2 * g + 2, :])
        o_ref[0, :, g * LANES:(g + 1) * LANES] = (acc_scr[g * LANES:(g + 1) * LANES, :] / l_pair).T


def _dsa_prompt(q, iq, ikwt, ikbf, kbf, vtbf, k_sel, qb):
    b, t, _ = q.shape
    blk = lambda w_: pl.BlockSpec((1, qb, w_), lambda i, j: (i, j, 0))
    full = lambda w_: pl.BlockSpec((1, t, w_), lambda i, j: (i, 0, 0))
    return pl.pallas_call(
        functools.partial(_dsa_prompt_kernel, k_sel), grid=(b, t // qb),
        in_specs=[blk(A_WIDTH), blk(IDX_HEADS * IDX_DIM),
                  pl.BlockSpec((1, SUBLANES, qb), lambda i, j: (i, IDX_DIM // SUBLANES, j)),
                  full(IDX_DIM), full(A_WIDTH), pl.BlockSpec((1, A_WIDTH, t), lambda i, j: (i, 0, 0))],
        out_specs=blk(A_WIDTH), out_shape=jax.ShapeDtypeStruct((b, t, A_WIDTH), F32),
        scratch_shapes=[pltpu.VMEM((t, qb), I32), pltpu.VMEM((t, qb), F32),
                        pltpu.VMEM((A_WIDTH // LANES, 2 * qb, LANES), BF16), pltpu.VMEM((A_HEADS, qb), F32),
                        pltpu.VMEM((A_HEADS, qb), F32), pltpu.VMEM((A_WIDTH, qb), F32)],
        compiler_params=_cparams(("parallel", "arbitrary"), _mib(48)), name="dsa_prompt",
    )(q, iq, ikwt, ikbf, kbf, vtbf)


S5_SEG = 64
S5_CHUNK = S5_SEG * SUBLANES


def _s5_prompt_kernel(u_ref, bblk_ref, cblk_ref, d_ref, a_ref, h0_ref, y_ref, hout_ref,
                      h_scr, pw_scr, hin_scr, carry_scr):
    c = pl.program_id(1)
    a = a_ref[...]
    a_re1, a_im1 = a[:, :S5_FLAT], a[:, S5_FLAT:]

    @pl.when(c == 0)
    def _():
        carry_scr[...] = h0_ref[0]
        pr, pi = a_re1, a_im1
        for k in range(S5_SEG):
            pw_scr[k:k + 1, :S5_FLAT] = pr
            pw_scr[k:k + 1, S5_FLAT:] = pi
            pr, pi = pr * a_re1 - pi * a_im1, pr * a_im1 + pi * a_re1

    up = u_ref[0]
    h_scr[...] = _dot(up, bblk_ref[...])

    a_re = jnp.broadcast_to(a_re1, (SUBLANES, S5_FLAT))
    a_im = jnp.broadcast_to(a_im1, (SUBLANES, S5_FLAT))

    def scan_step(k, carry):
        hr, hi = carry
        rows = pl.ds(pl.multiple_of(k * SUBLANES, SUBLANES), SUBLANES)
        nhr = a_re * hr - a_im * hi + h_scr[rows, :S5_FLAT]
        nhi = a_re * hi + a_im * hr + h_scr[rows, S5_FLAT:]
        h_scr[rows, :S5_FLAT] = nhr
        h_scr[rows, S5_FLAT:] = nhi
        return nhr, nhi

    zero = jnp.zeros((SUBLANES, S5_FLAT), F32)
    end_re, end_im = lax.fori_loop(0, S5_SEG, scan_step, (zero, zero))

    p_last_re, p_last_im = pw_scr[S5_SEG - 1:S5_SEG, :S5_FLAT], pw_scr[S5_SEG - 1:S5_SEG, S5_FLAT:]
    cr, ci = carry_scr[:, :S5_FLAT], carry_scr[:, S5_FLAT:]
    for j in range(SUBLANES):
        hin_scr[j:j + 1, :S5_FLAT] = cr
        hin_scr[j:j + 1, S5_FLAT:] = ci
        cr, ci = (end_re[j:j + 1] + p_last_re * cr - p_last_im * ci,
                  end_im[j:j + 1] + p_last_re * ci + p_last_im * cr)
    carry_scr[:, :S5_FLAT] = cr
    carry_scr[:, S5_FLAT:] = ci
    hin_re, hin_im = hin_scr[:, :S5_FLAT], hin_scr[:, S5_FLAT:]

    def fix_step(k, _):
        rows = pl.ds(pl.multiple_of(k * SUBLANES, SUBLANES), SUBLANES)
        pr = jnp.broadcast_to(pw_scr[pl.ds(k, 1), :S5_FLAT], (SUBLANES, S5_FLAT))
        pi = jnp.broadcast_to(pw_scr[pl.ds(k, 1), S5_FLAT:], (SUBLANES, S5_FLAT))
        h_scr[rows, :S5_FLAT] = h_scr[rows, :S5_FLAT] + pr * hin_re - pi * hin_im
        h_scr[rows, S5_FLAT:] = h_scr[rows, S5_FLAT:] + pr * hin_im + pi * hin_re
        return 0

    lax.fori_loop(0, S5_SEG, fix_step, 0)
    y_ref[0] = _dot(h_scr[...], cblk_ref[...]) + d_ref[...] * up

    @pl.when(c == pl.num_programs(1) - 1)
    def _():
        hout_ref[0] = carry_scr[...]


def _s5_prompt(u, bblk, cblk, dvec, avec, h0):
    b, t, _ = u.shape
    blk = pl.BlockSpec((1, S5_CHUNK, S5_WIDTH), lambda i, j: (i, j, 0))
    st = pl.BlockSpec((1, 1, 2 * S5_FLAT), lambda i, j: (i, 0, 0))
    return pl.pallas_call(
        _s5_prompt_kernel, grid=(b, t // S5_CHUNK),
        in_specs=[blk, _const_spec((S5_WIDTH, 2 * S5_FLAT)), _const_spec((2 * S5_FLAT, S5_WIDTH)),
                  _const_spec((1, S5_WIDTH)), _const_spec((1, 2 * S5_FLAT)), st],
        out_specs=[blk, st],
        out_shape=[jax.ShapeDtypeStruct((b, t, S5_WIDTH), F32), jax.ShapeDtypeStruct((b, 1, 2 * S5_FLAT), F32)],
        scratch_shapes=[pltpu.VMEM((S5_CHUNK, 2 * S5_FLAT), F32),
                        pltpu.VMEM((S5_SEG, 2 * S5_FLAT), F32), pltpu.VMEM((SUBLANES, 2 * S5_FLAT), F32),
                        pltpu.VMEM((1, 2 * S5_FLAT), F32)],
        compiler_params=_cparams(("parallel", "arbitrary"), _mib(32)), name="s5_prompt",
    )(u, bblk, cblk, dvec, avec, h0)


def _sgate_prompt_kernel(zc_ref, g_ref, b_ref, w_ref, bias_ref, o_ref):
    z = jax.nn.gelu(zc_ref[...])
    u = z[:, :SG_WIDTH]
    v = _layer_norm(z[:, SG_WIDTH:], g_ref[...], b_ref[...])
    ri = lax.broadcasted_iota(I32, (SG_CHUNK, SG_CHUNK), 0)
    ci = lax.broadcasted_iota(I32, (SG_CHUNK, SG_CHUNK), 1)
    wm = [jnp.where(ci <= ri, w_ref[g], 0.0).astype(BF16) for g in range(SG_GROUPS)]
    low = lax.broadcasted_iota(I32, (SG_CHUNK, LANES), 1) < SG_GROUP_CH
    bias = bias_ref[...]
    for c in range(zc_ref.shape[0] // SG_CHUNK):
        rows = slice(c * SG_CHUNK, (c + 1) * SG_CHUNK)
        for pr in range(SG_WIDTH // LANES):
            cols = slice(pr * LANES, (pr + 1) * LANES)
            vp = v[rows, cols].astype(BF16)
            s = jnp.where(low, _dot(wm[2 * pr], vp), _dot(wm[2 * pr + 1], vp)) + bias[:, cols]
            o_ref[rows, cols] = u[rows, cols] * s


def _sgate_prompt(zc, g, b, w, bias, tm):
    n = zc.shape[0]
    return pl.pallas_call(
        _sgate_prompt_kernel, grid=(n // tm,),
        in_specs=[pl.BlockSpec((tm, 2 * SG_WIDTH), lambda i: (i, 0)), _const_spec((1, SG_WIDTH)),
                  _const_spec((1, SG_WIDTH)), _const_spec((SG_GROUPS, SG_CHUNK, SG_CHUNK)),
                  _const_spec((SG_CHUNK, SG_WIDTH))],
        out_specs=pl.BlockSpec((tm, SG_WIDTH), lambda i: (i, 0)),
        out_shape=jax.ShapeDtypeStruct((n, SG_WIDTH), F32),
        compiler_params=_cparams(("parallel",), _mib(24)), name="sgate_prompt",
    )(zc, g, b, w, bias)


def _merge_kernel(alpha, x_ref, att_ref, y5_ref, sg_ref, wa_ref, wga_ref, wgb_ref, wc_ref, wg_ref, bg_ref, wmix_ref,
                  g_ref, b_ref, o_ref):
    x = x_ref[...]
    gates = jax.nn.sigmoid(_dot(x, wg_ref[...]) + bg_ref[...])
    br_a = _dot(att_ref[...], wa_ref[...])
    y5 = jax.nn.gelu(y5_ref[...])
    br_b = _dot(y5, wga_ref[...]) * jax.nn.sigmoid(_dot(y5, wgb_ref[...]))
    br_c = _dot(sg_ref[...], wc_ref[...])
    merged = (gates[:, :D_MODEL] * br_a + gates[:, D_MODEL:2 * D_MODEL] * br_b + gates[:, 2 * D_MODEL:] * br_c)
    o_ref[...] = _layer_norm(alpha * x + _dot(merged, wmix_ref[...]), g_ref[...], b_ref[...])


def _merge(alpha, x, att, y5, sg, wa, wga, wgb, wc, wg, bg, wmix, g, b, tm):
    n = x.shape[0]
    row = lambda w_: pl.BlockSpec((tm, w_), lambda i: (i, 0))
    return pl.pallas_call(
        functools.partial(_merge_kernel, alpha), grid=(n // tm,),
        in_specs=[row(D_MODEL), row(A_WIDTH), row(S5_WIDTH), row(SG_WIDTH),
                  _const_spec((A_WIDTH, D_MODEL)), _const_spec((S5_WIDTH, D_MODEL)), _const_spec((S5_WIDTH, D_MODEL)),
                  _const_spec((SG_WIDTH, D_MODEL)), _const_spec((D_MODEL, N_BRANCH * D_MODEL)),
                  _const_spec((1, N_BRANCH * D_MODEL)), _const_spec((D_MODEL, D_MODEL)),
                  _const_spec((1, D_MODEL)), _const_spec((1, D_MODEL))],
        out_specs=row(D_MODEL), out_shape=jax.ShapeDtypeStruct((n, D_MODEL), F32),
        compiler_params=_cparams(("parallel",), _mib(48)), name="branch_merge",
    )(x, att, y5, sg, wa, wga, wgb, wc, wg, bg, wmix, g, b)


def _two_linear_kernel(x_ref, w1_ref, w2_ref, o1_ref, o2_ref):
    x = x_ref[...]
    o1_ref[...] = _dot(x, w1_ref[...])
    o2_ref[...] = _dot(x, w2_ref[...])


def _two_linear(x, w1, w2, tm):
    n, k = x.shape
    m = w1.shape[1]
    return pl.pallas_call(
        _two_linear_kernel, grid=(n // tm,),
        in_specs=[pl.BlockSpec((tm, k), lambda i: (i, 0)), _const_spec((k, m)), _const_spec((k, m))],
        out_specs=[pl.BlockSpec((tm, m), lambda i: (i, 0))] * 2,
        out_shape=[jax.ShapeDtypeStruct((n, m), F32)] * 2,
        compiler_params=_cparams(("parallel",), _mib(24)), name="mem_kv_proj",
    )(x, w1, w2)


def _mem_prompt_kernel(alpha, x_ref, mk_ref, mv_ref, wq_ref, wo_ref, g_ref, b_ref, o_ref):
    x = x_ref[0]
    q = _dot(x, wq_ref[...])
    mk = mk_ref[0].astype(BF16)
    mv = mv_ref[0].astype(BF16)
    outs = []
    for h in range(MEM_HEADS):
        sl = slice(h * MEM_HEAD_DIM, (h + 1) * MEM_HEAD_DIM)
        s = _dot_nt(q[:, sl], mk[:, sl]) * (MEM_HEAD_DIM ** -0.5)
        m = jnp.max(s, axis=1, keepdims=True)
        p = jnp.exp(s - m)
        l = jnp.sum(p, axis=1, keepdims=True)
        outs.append(_dot(p, mv[:, sl]) / l)
    o = jnp.concatenate(outs, axis=1)
    o_ref[0] = _layer_norm(alpha * x + _dot(o, wo_ref[...]), g_ref[...], b_ref[...])


def _mem_prompt(alpha, x, mk, mv, wq, wo, g, b, tm):
    bsz, t, _ = x.shape
    return pl.pallas_call(
        functools.partial(_mem_prompt_kernel, alpha), grid=(bsz, t // tm),
        in_specs=[pl.BlockSpec((1, tm, D_MODEL), lambda i, j: (i, j, 0)),
                  pl.BlockSpec((1, MEM_TOKENS, MEM_WIDTH), lambda i, j: (i, 0, 0)),
                  pl.BlockSpec((1, MEM_TOKENS, MEM_WIDTH), lambda i, j: (i, 0, 0)),
                  _const_spec((D_MODEL, MEM_WIDTH)), _const_spec((MEM_WIDTH, D_MODEL)),
                  _const_spec((1, D_MODEL)), _const_spec((1, D_MODEL))],
        out_specs=pl.BlockSpec((1, tm, D_MODEL), lambda i, j: (i, j, 0)),
        out_shape=jax.ShapeDtypeStruct((bsz, t, D_MODEL), F32),
        compiler_params=_cparams(("parallel", "parallel"), _mib(32)), name="mem_attend_prompt",
    )(x, mk, mv, wq, wo, g, b)


def _ffn_kernel(alpha, x_ref, w1_ref, w2_ref, g_ref, b_ref, o_ref):
    x = x_ref[...]
    hid = jnp.square(jnp.maximum(_dot(x, w1_ref[...]), 0.0))
    o_ref[...] = _layer_norm(alpha * x + _dot(hid, w2_ref[...]), g_ref[...], b_ref[...])


def _ffn(alpha, x, w1, w2, g, b, tm):
    n = x.shape[0]
    row = pl.BlockSpec((tm, D_MODEL), lambda i: (i, 0))
    return pl.pallas_call(
        functools.partial(_ffn_kernel, alpha), grid=(n // tm,),
        in_specs=[row, _const_spec((D_MODEL, D_FF)), _const_spec((D_FF, D_MODEL)),
                  _const_spec((1, D_MODEL)), _const_spec((1, D_MODEL))],
        out_specs=row, out_shape=jax.ShapeDtypeStruct((n, D_MODEL), F32),
        compiler_params=_cparams(("parallel",), _mib(56)), name="relu2_mlp",
    )(x, w1, w2, g, b)


def _sample_s5_sg_kernel(u_ref, zc_ref, h0_ref, bblk_ref, cblk_ref, d_ref, a_ref, g_ref, b_ref, wd_ref, bd_ref,
                         y_ref, h_ref, sg_ref, vsg_ref):
    u = u_ref[...]
    a = a_ref[...]
    a_re, a_im = a[:, :S5_FLAT], a[:, S5_FLAT:]
    h0 = h0_ref[...]
    h0_re, h0_im = h0[:, :S5_FLAT], h0[:, S5_FLAT:]
    bu = _dot(u, bblk_ref[...])
    h_re = bu[:, :S5_FLAT] + (a_re * h0_re - a_im * h0_im)
    h_im = bu[:, S5_FLAT:] + (a_re * h0_im + a_im * h0_re)
    h_ref[:, :S5_FLAT] = h_re
    h_ref[:, S5_FLAT:] = h_im
    y_ref[...] = _dot(h_ref[...], cblk_ref[...]) + d_ref[...] * u
    z = jax.nn.gelu(zc_ref[...])
    v = _layer_norm(z[:, SG_WIDTH:], g_ref[...], b_ref[...])
    vsg_ref[...] = v
    sg_ref[...] = z[:, :SG_WIDTH] * (wd_ref[...] * v + bd_ref[...])


def _sample_s5_sg(u, zc, h0, bblk, cblk, dvec, avec, g, b, wd, bd):
    n = u.shape[0]
    full = lambda a: _const_spec(a.shape)
    args = (u, zc, h0, bblk, cblk, dvec, avec, g, b, wd, bd)
    return pl.pallas_call(
        _sample_s5_sg_kernel, grid=(1,), in_specs=[full(a) for a in args],
        out_specs=[_const_spec((n, S5_WIDTH)), _const_spec((n, 2 * S5_FLAT)), _const_spec((n, SG_WIDTH)),
                   _const_spec((n, SG_WIDTH))],
        out_shape=[jax.ShapeDtypeStruct((n, S5_WIDTH), F32), jax.ShapeDtypeStruct((n, 2 * S5_FLAT), F32),
                   jax.ShapeDtypeStruct((n, SG_WIDTH), F32), jax.ShapeDtypeStruct((n, SG_WIDTH), F32)],
        compiler_params=_cparams(("arbitrary",), _mib(24)), name="sample_s5_sgate",
    )(*args)


def _sample_scores_kernel(n_pages, pt_ref, iq_ref, iw_ref, ikw_ref, *refs):
    page_refs, o_ref = refs[:n_pages], refs[n_pages]
    iq = iq_ref[0]
    iw = iw_ref[0] * (IDX_HEADS ** -0.5)
    for j in range(n_pages):
        dots = _dot(iq, page_refs[j][0, 0]) * (IDX_DIM ** -0.5)
        o_ref[0, :, j * PAGE_SIZE:(j + 1) * PAGE_SIZE] = jnp.sum(iw * jnp.maximum(dots, 0.0), axis=0, keepdims=True)
    own = jnp.sum(iq * ikw_ref[0][:, :IDX_DIM], axis=1, keepdims=True) * (IDX_DIM ** -0.5)
    own = jnp.sum(iw * jnp.maximum(own, 0.0), axis=0, keepdims=True)
    o_ref[0, :, n_pages * PAGE_SIZE:] = jnp.broadcast_to(own, (1, LANES))


def _sample_scores(page_table, iq, iw, ikw, cache_kidx_t, layer):
    n, n_pages = page_table.shape
    width = n_pages * PAGE_SIZE + LANES
    page = lambda j: pl.BlockSpec((1, 1, IDX_DIM, PAGE_SIZE), lambda i, pt: (layer, pt[i, j], 0, 0))
    grid_spec = pltpu.PrefetchScalarGridSpec(
        num_scalar_prefetch=1, grid=(n,),
        in_specs=[pl.BlockSpec((1, SUBLANES, IDX_DIM), lambda i, pt: (i, 0, 0)),
                  pl.BlockSpec((1, SUBLANES, 1), lambda i, pt: (i, 0, 0)),
                  pl.BlockSpec((1, 1, LANES), lambda i, pt: (i, 0, 0))] + [page(j) for j in range(n_pages)],
        out_specs=pl.BlockSpec((1, 1, width), lambda i, pt: (i, 0, 0)))
    return pl.pallas_call(
        functools.partial(_sample_scores_kernel, n_pages), grid_spec=grid_spec,
        out_shape=jax.ShapeDtypeStruct((n, 1, width), F32),
        compiler_params=_cparams(("arbitrary",), _mib(24)), name="sample_idx_scores",
    )(page_table, iq, iw, ikw, *([cache_kidx_t] * n_pages))


def _sample_select_kernel(k_sel, sct_ref, o_ref, key_scr):
    past, r = key_scr.shape
    n_chunks = past // KEY_CHUNK
    key_scr[...] = _score_keys(sct_ref[:past, :])
    xkey = _score_keys(sct_ref[past:past + 1, :])
    thr, jstar = _topk_threshold(key_scr, n_chunks, k_sel, past, xkey)
    row = lax.broadcasted_iota(I32, (KEY_CHUNK, r), 0)
    for c in range(n_chunks):
        rows = slice(c * KEY_CHUNK, (c + 1) * KEY_CHUNK)
        o_ref[rows, :] = jnp.where(_selected(key_scr[rows, :], row + c * KEY_CHUNK, thr, jstar), 1.0, 0.0)
    own = jnp.where(_selected(xkey, jnp.full((1, r), past, I32), thr, jstar), 1.0, 0.0)
    o_ref[past:, :] = jnp.broadcast_to(own, (o_ref.shape[0] - past, r))


def _sample_select(scores_t, k_sel):
    rows, n = scores_t.shape
    past = rows - LANES
    assert past % KEY_CHUNK == 0
    return pl.pallas_call(
        functools.partial(_sample_select_kernel, k_sel), grid=(1,),
        in_specs=[_const_spec(scores_t.shape)], out_specs=_const_spec(scores_t.shape),
        out_shape=jax.ShapeDtypeStruct(scores_t.shape, F32),
        scratch_shapes=[pltpu.VMEM((past, n), I32)],
        compiler_params=_cparams(("arbitrary",), _mib(24)), name="sample_topk_mask",
    )(scores_t)


def _sample_attend_kernel(n_pages, pt_ref, q_ref, kn_ref, vn_ref, m_ref, *refs):
    k_refs, v_refs, o_ref = refs[:n_pages], refs[n_pages:2 * n_pages], refs[2 * n_pages]
    q8 = q_ref[0] * (A_HEAD_DIM ** -0.5)
    q8b = q8.astype(BF16)
    head_of_row = lax.broadcasted_iota(I32, (A_HEADS, PAGE_SIZE), 0)
    pages = []
    for j in range(n_pages):
        sj = jnp.zeros((A_HEADS, PAGE_SIZE), F32)
        for h in range(A_HEADS):
            sj = jnp.where(head_of_row == h, _dot(q8b, k_refs[j][0, 0, h]), sj)
        pages.append(jnp.where(m_ref[0, :, j * PAGE_SIZE:(j + 1) * PAGE_SIZE] > 0.5, sj, -jnp.inf))
    s = jnp.concatenate(pages, axis=1)
    s_own = jnp.sum(q8 * kn_ref[0], axis=1, keepdims=True)
    s_own = jnp.where(m_ref[0, :, n_pages * PAGE_SIZE:n_pages * PAGE_SIZE + 1] > 0.5, s_own, -jnp.inf)
    m = jnp.maximum(jnp.max(s, axis=1, keepdims=True), s_own)
    p = jnp.exp(s - m)
    p_own = jnp.exp(s_own - m)
    l = jnp.sum(p, axis=1, keepdims=True) + p_own
    acc = p_own * vn_ref[0]
    head_of_out = lax.broadcasted_iota(I32, (A_HEADS, A_HEAD_DIM), 0)
    for j in range(n_pages):
        pj = p[:, j * PAGE_SIZE:(j + 1) * PAGE_SIZE].astype(BF16)
        for h in range(A_HEADS):
            acc = acc + jnp.where(head_of_out == h, _dot_nt(pj, v_refs[j][0, 0, h]), 0.0)
    o_ref[0] = acc / l


def _sample_attend(page_table, q, k_new, v_new, mask, cache_k_t, cache_v_t, layer):
    n, n_pages = page_table.shape
    page = lambda j: pl.BlockSpec((1, 1, A_HEADS, A_HEAD_DIM, PAGE_SIZE), lambda i, pt: (layer, pt[i, j], 0, 0, 0))
    tok = pl.BlockSpec((1, A_HEADS, A_HEAD_DIM), lambda i, pt: (i, 0, 0))
    grid_spec = pltpu.PrefetchScalarGridSpec(
        num_scalar_prefetch=1, grid=(n,),
        in_specs=[tok, tok, tok, pl.BlockSpec((1, 1, mask.shape[-1]), lambda i, pt: (i, 0, 0))]
        + [page(j) for j in range(n_pages)] * 2,
        out_specs=tok)
    return pl.pallas_call(
        functools.partial(_sample_attend_kernel, n_pages), grid_spec=grid_spec,
        out_shape=jax.ShapeDtypeStruct((n, A_HEADS, A_HEAD_DIM), F32),
        compiler_params=_cparams(("arbitrary",), _mib(40)), name="sample_dsa_attend",
    )(page_table, q, k_new, v_new, mask, *([cache_k_t] * n_pages), *([cache_v_t] * n_pages))


def _linear_kernel(x_ref, w_ref, o_ref):
    o_ref[...] = _dot(x_ref[...], w_ref[...])


def _linear(x, w):
    n, m = x.shape[0], w.shape[1]
    return pl.pallas_call(
        _linear_kernel, grid=(1,), in_specs=[_const_spec(x.shape), _const_spec(w.shape)],
        out_specs=_const_spec((n, m)), out_shape=jax.ShapeDtypeStruct((n, m), F32),
        compiler_params=_cparams(("arbitrary",), _mib(24)), name="linear",
    )(x, w)


def _linear_res_ln_kernel(alpha, o_ref, w_ref, x_ref, g_ref, b_ref, y_ref):
    y_ref[...] = _layer_norm(alpha * x_ref[...] + _dot(o_ref[...], w_ref[...]), g_ref[...], b_ref[...])


def _linear_res_ln(alpha, o, w, x, g, b):
    args = (o, w, x, g, b)
    return pl.pallas_call(
        functools.partial(_linear_res_ln_kernel, alpha), grid=(1,), in_specs=[_const_spec(a.shape) for a in args],
        out_specs=_const_spec(x.shape), out_shape=jax.ShapeDtypeStruct(x.shape, F32),
        compiler_params=_cparams(("arbitrary",), _mib(24)), name="linear_residual_norm",
    )(*args)


def _mem_sample_kernel(bs, q_ref, mk_ref, mv_ref, o_ref):
    for i in range(bs):
        q = q_ref[i]
        s = jnp.sum(mk_ref[0, i] * q[None], axis=-1, keepdims=True) * (MEM_HEAD_DIM ** -0.5)
        m = jnp.max(s, axis=0, keepdims=True)
        p = jnp.exp(s - m)
        l = jnp.sum(p, axis=0)
        o_ref[i] = jnp.sum(p * mv_ref[0, i], axis=0) / l


def _mem_sample(q, cache_mk, cache_mv, layer, bs):
    n = q.shape[0]
    cache = pl.BlockSpec((1, bs, MEM_TOKENS, MEM_HEADS, MEM_HEAD_DIM), lambda i: (layer, i, 0, 0, 0))
    tok = pl.BlockSpec((bs, MEM_HEADS, MEM_HEAD_DIM), lambda i: (i, 0, 0))
    return pl.pallas_call(
        functools.partial(_mem_sample_kernel, bs), grid=(n // bs,),
        in_specs=[tok, cache, cache], out_specs=tok,
        out_shape=jax.ShapeDtypeStruct((n, MEM_HEADS, MEM_HEAD_DIM), F32),
        compiler_params=_cparams(("parallel",), _mib(40)), name="mem_attend_sample",
    )(q, cache_mk, cache_mv)


def _pack_w_in(w_in):
    lead = w_in.shape[:-1]
    pts = np.cumsum([A_WIDTH, A_WIDTH, A_WIDTH, IDX_HEADS * IDX_DIM, IDX_DIM, IDX_HEADS, S5_WIDTH]).tolist()
    q, k, v, iq, ik, iw, u5, zc = jnp.split(w_in, pts, axis=-1)
    pad = jnp.zeros(lead + (IKW_PAD,), w_in.dtype)
    return jnp.concatenate([q, k, v, iq, ik, iw, pad, u5, zc], axis=-1).astype(BF16)


def _rope_tables(pos):
    half = A_HEAD_DIM // 2
    inv_freq = ROPE_THETA ** (-jnp.arange(half, dtype=F32) / half)
    ang = pos.astype(F32)[:, None] * inv_freq[None, :]
    c, s = jnp.cos(ang), jnp.sin(ang)
    return jnp.tile(c, (1, LANES // half)), jnp.concatenate([-s, s] * (LANES // A_HEAD_DIM), axis=1)


def _s5_matrices(lam_re, lam_im, log_dt, b_re, b_im, c_re, c_im, d):
    dt = jnp.exp(log_dt)[:, None]
    mag = jnp.exp(dt * lam_re)
    ang = dt * lam_im
    a_re = mag * jnp.cos(ang)
    a_im = mag * jnp.sin(ang)
    den = jnp.square(lam_re) + jnp.square(lam_im)
    z_re = ((a_re - 1.0) * lam_re + a_im * lam_im) / den
    z_im = (a_im * lam_re - (a_re - 1.0) * lam_im) / den
    bb_re = z_re[..., None] * b_re - z_im[..., None] * b_im
    bb_im = z_re[..., None] * b_im + z_im[..., None] * b_re
    eye = jnp.eye(S5_GROUPS, dtype=F32)
    in_re = jnp.einsum("gph,gk->ghkp", bb_re, eye).reshape(S5_WIDTH, S5_FLAT)
    in_im = jnp.einsum("gph,gk->ghkp", bb_im, eye).reshape(S5_WIDTH, S5_FLAT)
    bblk = jnp.concatenate([in_re, in_im], axis=1).astype(BF16)
    out_re = jnp.einsum("ghp,gk->gpkh", c_re, eye).reshape(S5_FLAT, S5_WIDTH)
    out_im = jnp.einsum("ghp,gk->gpkh", -c_im, eye).reshape(S5_FLAT, S5_WIDTH)
    cblk = jnp.concatenate([out_re, out_im], axis=0).astype(BF16)
    avec = jnp.concatenate([a_re.reshape(1, S5_FLAT), a_im.reshape(1, S5_FLAT)], axis=1)
    return bblk, cblk, d.reshape(1, S5_WIDTH), avec


def _pick_tile(n, pref):
    t = min(n, pref)
    assert n % t == 0
    return t


def kernel(x_prompt, x_sample, cache_k, cache_v, cache_kidx, cache_mem_k, cache_mem_v, state_s5_re, state_s5_im,
           page_table, mem_prompt, w_in, w_a_out, s5_lambda_re, s5_lambda_im, s5_log_dt, s5_b_re, s5_b_im,
           s5_c_re, s5_c_im, s5_d, w_glu_a, w_glu_b, sg_ln_g, sg_ln_b, sg_w, sg_b, w_c_out, w_gate, b_gate,
           w_mix_out, ln1_g, ln1_b, w_mq, w_mk, w_mv, w_mo, ln2_g, ln2_b, w_ff1, w_ff2, ln3_g, ln3_b):
    depth = w_in.shape[0]
    bp, tp, _ = x_prompt.shape
    db, ds, _ = x_sample.shape
    assert ds == 1 and tp % S5_CHUNK == 0 and tp % SG_CHUNK == 0 and tp % KEY_CHUNK == 0
    n_pages = page_table.shape[1]
    past = n_pages * PAGE_SIZE
    alpha = (2 * depth) ** 0.25
    bf = lambda a: a.astype(BF16)
    row = lambda a, l: a[l][None, :]

    w_in_p = _pack_w_in(w_in)
    w_a_out_b, w_glu_a_b, w_glu_b_b, w_c_out_b = bf(w_a_out), bf(w_glu_a), bf(w_glu_b), bf(w_c_out)
    w_gate_b, w_mix_b = bf(w_gate), bf(w_mix_out)
    w_mq_b, w_mk_b, w_mv_b, w_mo_b = bf(w_mq), bf(w_mk), bf(w_mv), bf(w_mo)
    w_ff1_b, w_ff2_b = bf(w_ff1), bf(w_ff2)
    s5m = [_s5_matrices(s5_lambda_re[l], s5_lambda_im[l], s5_log_dt[l], s5_b_re[l], s5_b_im[l], s5_c_re[l],
                        s5_c_im[l], s5_d[l]) for l in range(depth)]

    n = bp * tp
    tm = _pick_tile(tp, 256)
    cos_p, sin_p = _rope_tables(jnp.arange(tp))
    k_sel_p = min(TOPK_MAX, tp // 4)
    qb = _pick_tile(tp, 128)
    sg_bias = [jnp.repeat(sg_b[l].T, SG_GROUP_CH, axis=1) for l in range(depth)]
    mem2d = mem_prompt.reshape(bp * MEM_TOKENS, D_MODEL)
    h0 = jnp.zeros((bp, 1, 2 * S5_FLAT), F32)

    h = x_prompt.reshape(n, D_MODEL)
    ktp, vtp, iktp, mkp, mvp, s5p = [], [], [], [], [], []
    for l in range(depth):
        bblk, cblk, dvec, avec = s5m[l]
        q, iq, u5, zc, kbf, ikbf, kt, vt, vtbf, ikwt = _proj(h, w_in_p[l], cos_p, sin_p, tp, tm, False)
        r3 = lambda a: a.reshape(bp, tp, a.shape[-1])
        att = _dsa_prompt(r3(q), r3(iq), ikwt, r3(ikbf), r3(kbf), vtbf, k_sel_p, qb)
        u5p = u5.reshape(bp, tp // S5_CHUNK, SUBLANES, S5_SEG, S5_WIDTH).swapaxes(2, 3).reshape(bp, tp, S5_WIDTH)
        y5p, hfin = _s5_prompt(u5p, bblk, cblk, dvec, avec, h0)
        y5 = y5p.reshape(bp, tp // S5_CHUNK, S5_SEG, SUBLANES, S5_WIDTH).swapaxes(2, 3)
        sg = _sgate_prompt(zc, row(sg_ln_g, l), row(sg_ln_b, l), sg_w[l], sg_bias[l], _pick_tile(tp, 512))
        h = _merge(alpha, h, att.reshape(n, A_WIDTH), y5.reshape(n, S5_WIDTH), sg, w_a_out_b[l], w_glu_a_b[l],
                   w_glu_b_b[l], w_c_out_b[l], w_gate_b[l], row(b_gate, l), w_mix_b[l], row(ln1_g, l), row(ln1_b, l), tm)
        mk, mv = _two_linear(mem2d, w_mk_b[l], w_mv_b[l], _pick_tile(bp * MEM_TOKENS, 256))
        h = _mem_prompt(alpha, h.reshape(bp, tp, D_MODEL), mk.reshape(bp, MEM_TOKENS, MEM_WIDTH),
                        mv.reshape(bp, MEM_TOKENS, MEM_WIDTH), w_mq_b[l], w_mo_b[l], row(ln2_g, l), row(ln2_b, l),
                        tm).reshape(n, D_MODEL)
        h = _ffn(alpha, h, w_ff1_b[l], w_ff2_b[l], row(ln3_g, l), row(ln3_b, l), tm)
        ktp.append(kt)
        vtp.append(vt)
        iktp.append(ikwt[:, :IDX_DIM, :])
        mkp.append(mk.reshape(bp, MEM_TOKENS, MEM_HEADS, MEM_HEAD_DIM))
        mvp.append(mv.reshape(bp, MEM_TOKENS, MEM_HEADS, MEM_HEAD_DIM))
        s5p.append(hfin.reshape(bp, 2, S5_GROUPS, S5_STATE))
    y_prompt = h.reshape(bp, tp, D_MODEL)

    cache_k_t = jnp.transpose(cache_k, (0, 1, 3, 4, 2))
    cache_v_t = jnp.transpose(cache_v, (0, 1, 3, 4, 2))
    cache_kidx_t = jnp.swapaxes(cache_kidx, 2, 3)
    cos_s, sin_s = _rope_tables(jnp.full((db,), past, jnp.int32))
    k_sel_s = min(TOPK_MAX, (past + ds) // 4)
    h = x_sample.reshape(db, D_MODEL)
    kts, vts, ikts, s5s, sgs = [], [], [], [], []
    for l in range(depth):
        bblk, cblk, dvec, avec = s5m[l]
        q, iq, u5, zc, _, _, kt, vt, _, ikwt, k, v, ikw = _proj(h, w_in_p[l], cos_s, sin_s, db, db, True)
        head_pad = ((0, 0), (0, SUBLANES - IDX_HEADS), (0, 0))
        scores = _sample_scores(page_table, jnp.pad(iq.reshape(db, IDX_HEADS, IDX_DIM), head_pad),
                                jnp.pad(ikw[:, IDX_DIM:IDX_DIM + IDX_HEADS].reshape(db, IDX_HEADS, 1), head_pad),
                                ikw.reshape(db, 1, LANES), cache_kidx_t, l)
        mask = _sample_select(scores.reshape(db, past + LANES).T, k_sel_s).T
        hd = lambda a: a.reshape(db, A_HEADS, A_HEAD_DIM)
        att = _sample_attend(page_table, hd(q), hd(k), hd(v), mask.reshape(db, 1, past + LANES), cache_k_t, cache_v_t, l)
        h0s = jnp.concatenate([state_s5_re[l].reshape(db, S5_FLAT), state_s5_im[l].reshape(db, S5_FLAT)], axis=1)
        wd = jnp.repeat(sg_w[l][:, 0, 0], SG_GROUP_CH)[None, :]
        bd = jnp.repeat(sg_b[l][:, 0], SG_GROUP_CH)[None, :]
        y5, hnew, sg, vsg = _sample_s5_sg(u5, zc, h0s, bblk, cblk, dvec, avec, row(sg_ln_g, l), row(sg_ln_b, l), wd, bd)
        h = _merge(alpha, h, att.reshape(db, A_WIDTH), y5, sg, w_a_out_b[l], w_glu_a_b[l], w_glu_b_b[l], w_c_out_b[l],
                   w_gate_b[l], row(b_gate, l), w_mix_b[l], row(ln1_g, l), row(ln1_b, l), db)
        mq = _linear(h, w_mq_b[l])
        mo = _mem_sample(mq.reshape(db, MEM_HEADS, MEM_HEAD_DIM), cache_mem_k, cache_mem_v, l, 4)
        h = _linear_res_ln(alpha, mo.reshape(db, MEM_WIDTH), w_mo_b[l], h, row(ln2_g, l), row(ln2_b, l))
        h = _ffn(alpha, h, w_ff1_b[l], w_ff2_b[l], row(ln3_g, l), row(ln3_b, l), db)
        kts.append(kt)
        vts.append(vt)
        ikts.append(ikwt[:, :IDX_DIM, :])
        s5s.append(hnew.reshape(db, 2, S5_GROUPS, S5_STATE))
        sgs.append(vsg.reshape(db, ds, SG_WIDTH))
    y_sample = h.reshape(db, ds, D_MODEL)

    def heads_last(stack, batch, seq):
        return jnp.transpose(stack.reshape(depth, batch, A_HEADS, A_HEAD_DIM, seq), (0, 1, 4, 2, 3))

    s5p, s5s = jnp.stack(s5p), jnp.stack(s5s)
    return (y_prompt, y_sample,
            heads_last(jnp.stack(ktp), bp, tp), heads_last(jnp.stack(vtp), bp, tp), jnp.swapaxes(jnp.stack(iktp), 2, 3),
            jnp.stack(mkp), jnp.stack(mvp), s5p[:, :, 0], s5p[:, :, 1],
            heads_last(jnp.stack(kts), 1, db).reshape(depth, db, ds, A_HEADS, A_HEAD_DIM),
            heads_last(jnp.stack(vts), 1, db).reshape(depth, db, ds, A_HEADS, A_HEAD_DIM),
            jnp.swapaxes(jnp.stack(ikts), 2, 3).reshape(depth, db, ds, IDX_DIM),
            s5s[:, :, 0], s5s[:, :, 1], jnp.stack(sgs))
```

```python
import functools
import math

import numpy as np
import jax
import jax.numpy as jnp
from jax import lax
from jax.experimental import pallas as pl
from jax.experimental.pallas import tpu as pltpu

F32 = jnp.float32
BF16 = jnp.bfloat16
I32 = jnp.int32

D_MODEL = 1024
A_HEADS = 8
A_HEAD_DIM = 64
A_WIDTH = A_HEADS * A_HEAD_DIM
IDX_HEADS = 4
IDX_DIM = 64
TOPK_MAX = 256
ROPE_THETA = 10000.0
S5_GROUP_CH = 16
S5_GROUPS = 16
S5_WIDTH = S5_GROUPS * S5_GROUP_CH
S5_STATE = 64
S5_FLAT = S5_GROUPS * S5_STATE
SG_CHUNK = 128
SG_GROUPS = 4
SG_WIDTH = 256
SG_GROUP_CH = SG_WIDTH // SG_GROUPS
MEM_TOKENS = 256
MEM_HEADS = 4
MEM_HEAD_DIM = 128
MEM_WIDTH = MEM_HEADS * MEM_HEAD_DIM
D_FF = 4 * D_MODEL
N_BRANCH = 3
LN_EPS = 1e-5
PAGE_SIZE = 128

LANES = 128
SUBLANES = 8
VMEM_PHYSICAL_BYTES = 64 * 1024 * 1024

IKW_PAD = LANES - IDX_DIM - IDX_HEADS
OFF_Q = 0
OFF_K = OFF_Q + A_WIDTH
OFF_V = OFF_K + A_WIDTH
OFF_IQ = OFF_V + A_WIDTH
OFF_IKW = OFF_IQ + IDX_HEADS * IDX_DIM
OFF_U5 = OFF_IKW + LANES
OFF_ZC = OFF_U5 + S5_WIDTH
IN_PACKED = OFF_ZC + 2 * SG_WIDTH

INT_MIN = -2 ** 31
NEG_INF_KEY = int(np.array([0xFF800000 ^ 0x7FFFFFFF], dtype=np.uint32).view(np.int32)[0])


def _cparams(semantics, vmem_bytes):
    assert vmem_bytes < VMEM_PHYSICAL_BYTES
    return pltpu.CompilerParams(dimension_semantics=semantics, vmem_limit_bytes=int(vmem_bytes))


def _mib(n):
    return n * 1024 * 1024


def _const_spec(shape):
    n = len(shape)
    return pl.BlockSpec(shape, lambda *_: (0,) * n)


def _layer_norm(x, g, b):
    mu = jnp.mean(x, axis=-1, keepdims=True)
    xc = x - mu
    var = jnp.mean(xc * xc, axis=-1, keepdims=True)
    return xc * lax.rsqrt(var + LN_EPS) * g + b


def _dot(a, b):
    return jnp.dot(a.astype(BF16), b.astype(BF16), preferred_element_type=F32)


def _dot_nt(a, b):
    return lax.dot_general(a.astype(BF16), b.astype(BF16), (((1,), (1,)), ((), ())),
                           preferred_element_type=F32)


def _proj_kernel(emit_rows, x_ref, w_ref, cos_ref, sin_ref, q_ref, iq_ref, u5_ref, zc_ref, kbf_ref, ikbf_ref,
                 kt_ref, vt_ref, vtbf_ref, ikwt_ref, *row_refs):
    y = _dot(x_ref[...], w_ref[...])
    cos = cos_ref[...]
    sin = sin_ref[...]
    lane = lax.broadcasted_iota(I32, cos.shape, 1)
    first_half = (lane & (A_HEAD_DIM // 2)) == 0

    def rope(t, c, s):
        partner = jnp.where(first_half, pltpu.roll(t, LANES - A_HEAD_DIM // 2, 1), pltpu.roll(t, A_HEAD_DIM // 2, 1))
        return t * c + partner * s

    for g in range(A_WIDTH // LANES):
        sl = slice(g * LANES, (g + 1) * LANES)
        q_ref[:, sl] = rope(y[:, OFF_Q + g * LANES:OFF_Q + (g + 1) * LANES], cos, sin)
        kg = rope(y[:, OFF_K + g * LANES:OFF_K + (g + 1) * LANES], cos, sin)
        kbf_ref[:, sl] = kg.astype(BF16)
        kt_ref[0, sl, :] = kg.T
        vg = y[:, OFF_V + g * LANES:OFF_V + (g + 1) * LANES]
        vt = vg.T
        vt_ref[0, sl, :] = vt
        vtbf_ref[0, sl, :] = vt.astype(BF16)
        if emit_rows:
            row_refs[0][:, sl] = kg
            row_refs[1][:, sl] = vg
    for g in range(IDX_HEADS * IDX_DIM // LANES):
        iq_ref[:, g * LANES:(g + 1) * LANES] = rope(y[:, OFF_IQ + g * LANES:OFF_IQ + (g + 1) * LANES], cos, sin)
    is_key = lane < IDX_DIM
    ikw = rope(y[:, OFF_IKW:OFF_IKW + LANES], jnp.where(is_key, cos, 1.0), jnp.where(is_key, sin, 0.0))
    ikbf_ref[...] = ikw[:, :IDX_DIM].astype(BF16)
    ikwt_ref[0] = ikw.T
    if emit_rows:
        row_refs[2][...] = ikw
    u5_ref[...] = y[:, OFF_U5:OFF_U5 + S5_WIDTH]
    zc_ref[...] = y[:, OFF_ZC:OFF_ZC + 2 * SG_WIDTH]


def _proj(x, w, cos, sin, seq, tm, emit_rows):
    n = x.shape[0]
    nt = seq // tm
    row = lambda w_: pl.BlockSpec((tm, w_), lambda i: (i, 0))
    tab = pl.BlockSpec((tm, LANES), lambda i: (i % nt, 0))
    tr = lambda w_: pl.BlockSpec((1, w_, tm), lambda i: (i // nt, 0, i % nt))
    sds = jax.ShapeDtypeStruct
    row_out = [(A_WIDTH, F32), (IDX_HEADS * IDX_DIM, F32), (S5_WIDTH, F32), (2 * SG_WIDTH, F32), (A_WIDTH, BF16),
               (IDX_DIM, BF16)]
    tr_out = [(A_WIDTH, F32), (A_WIDTH, F32), (A_WIDTH, BF16), (LANES, F32)]
    extra = [(A_WIDTH, F32), (A_WIDTH, F32), (LANES, F32)] if emit_rows else []
    out_shape = ([sds((n, w_), dt) for w_, dt in row_out] + [sds((n // seq, w_, seq), dt) for w_, dt in tr_out]
                 + [sds((n, w_), dt) for w_, dt in extra])
    out_specs = [row(w_) for w_, _ in row_out] + [tr(w_) for w_, _ in tr_out] + [row(w_) for w_, _ in extra]
    return pl.pallas_call(
        functools.partial(_proj_kernel, emit_rows), grid=(n // tm,),
        in_specs=[row(D_MODEL), _const_spec((D_MODEL, IN_PACKED)), tab, tab],
        out_specs=out_specs, out_shape=out_shape,
        compiler_params=_cparams(("parallel",), _mib(40)), name="in_proj_rope",
    )(x, w, cos, sin)


KEY_CHUNK = 512


def _score_keys(score):
    score = jnp.where(score == 0.0, 0.0, score)
    bits = lax.bitcast_convert_type(score, I32)
    return jnp.where(bits < 0, bits ^ 0x7FFFFFFF, bits)


def _rows_reduce(v, op):
    parts = [v[i * SUBLANES:(i + 1) * SUBLANES] for i in range(v.shape[0] // SUBLANES)]
    while len(parts) > 1:
        parts = [op(parts[i], parts[i + 1]) for i in range(0, len(parts), 2)]
    return parts[0]


def _chunk_rows(c):
    return pl.ds(pl.multiple_of(c * KEY_CHUNK, KEY_CHUNK), KEY_CHUNK)


def _count(key_ref, n_chunks, pred, extra=None):
    r = key_ref.shape[1]
    tile = SUBLANES * (4 // key_ref.dtype.itemsize)

    def body(c, acc):
        v = pred(key_ref[_chunk_rows(c), :], c * KEY_CHUNK)
        parts = [v[i * tile:(i + 1) * tile] for i in range(KEY_CHUNK // tile)]
        while len(parts) > 1:
            parts = [parts[i] + parts[i + 1] for i in range(0, len(parts), 2)]
        return acc + parts[0]

    acc = lax.fori_loop(0, n_chunks, body, jnp.zeros((tile, r), key_ref.dtype))
    cnt = jnp.sum(acc.astype(I32), axis=0, keepdims=True)
    return cnt if extra is None else cnt + extra


def _as_int(mask):
    return jnp.where(mask, 1, 0)


def _as_i16(mask):
    return jnp.where(mask, jnp.int16(1), jnp.int16(0))


HALF = 1 << 15


def _search16(half_ref, n_chunks, need, xhalf=None):
    r = half_ref.shape[1]

    def bit_body(i, prefix):
        cand = (prefix | jnp.left_shift(jnp.int32(1), 15 - i)) - HALF
        cand16 = cand.astype(jnp.int16)
        cnt = _count(half_ref, n_chunks, lambda ch, _: _as_i16(ch >= cand16),
                     None if xhalf is None else _as_int(xhalf >= cand))
        return jnp.where(cnt >= need, cand + HALF, prefix)

    return lax.fori_loop(0, 16, bit_body, jnp.zeros((1, r), I32)) - HALF


def _topk_threshold(key_ref, half_ref, n_chunks, k_sel, last_row, xkey=None):
    r = key_ref.shape[1]

    def split(k):
        return jnp.right_shift(k, 16), (k & 0xFFFF) - HALF

    def fill_hi(c, carry):
        half_ref[_chunk_rows(c), :] = split(key_ref[_chunk_rows(c), :])[0].astype(jnp.int16)
        return carry

    lax.fori_loop(0, n_chunks, fill_hi, 0)
    xhi, xlo = (None, None) if xkey is None else split(xkey)
    t_hi = _search16(half_ref, n_chunks, k_sel, xhi)
    t_hi16 = t_hi.astype(jnp.int16)
    above = _count(half_ref, n_chunks, lambda ch, _: _as_i16(ch > t_hi16), None if xkey is None else _as_int(xhi > t_hi))

    def fill_lo(c, carry):
        hi, lo = split(key_ref[_chunk_rows(c), :])
        half_ref[_chunk_rows(c), :] = jnp.where(hi == t_hi, lo, -HALF).astype(jnp.int16)
        return carry

    lax.fori_loop(0, n_chunks, fill_lo, 0)
    t_lo = _search16(half_ref, n_chunks, k_sel - above, None if xkey is None else jnp.where(xhi == t_hi, xlo, -HALF))
    thr = t_hi * (2 * HALF) + (t_lo + HALF)

    cnt_gt = _count(key_ref, n_chunks, lambda ch, _: _as_int(ch > thr), None if xkey is None else _as_int(xkey > thr))
    cnt_eq = _count(key_ref, n_chunks, lambda ch, _: _as_int(ch == thr), None if xkey is None else _as_int(xkey == thr))
    need = k_sel - cnt_gt
    surplus = jnp.where(cnt_eq > need, _as_int(thr > NEG_INF_KEY), 0)
    any_surplus = jnp.max(surplus) > 0
    row = lax.broadcasted_iota(I32, (KEY_CHUNK, r), 0)
    n_bits = int(last_row).bit_length()

    def tie_search():
        def body(i, lo):
            cand = lo + jnp.left_shift(jnp.int32(1), n_bits - 1 - i)
            extra = None if xkey is None else jnp.where(xkey == thr, _as_int(cand >= last_row), 0)
            cnt = _count(key_ref, n_chunks, lambda ch, r0: jnp.where(ch == thr, _as_int(row + r0 <= cand), 0), extra)
            return jnp.where(cnt < need, cand, lo)

        return lax.fori_loop(0, n_bits, body, jnp.full((1, r), -1, I32)) + 1

    jstar = lax.cond(any_surplus, tie_search, lambda: jnp.full((1, r), last_row, I32))
    return thr, jstar


def _selected(ch, row, thr, jstar):
    return jnp.where(ch == thr, _as_int(row <= jstar), _as_int(ch > thr)) > 0


def _dsa_prompt_kernel(k_sel, q_ref, iq_ref, iwt_ref, ikbf_ref, kbf_ref, vtbf_ref, o_ref, key_scr, half_scr, bias_scr,
                       qpair_scr, m_scr, l_scr, acc_scr):
    qb = q_ref.shape[1]
    t = kbf_ref.shape[1]
    q0 = pl.program_id(1) * qb
    n_chunks = (q0 + qb + KEY_CHUNK - 1) // KEY_CHUNK
    qcol = q0 + lax.broadcasted_iota(I32, (1, qb), 1)
    row = lax.broadcasted_iota(I32, (KEY_CHUNK, qb), 0)
    iwt = iwt_ref[0][:IDX_HEADS] * (IDX_HEADS ** -0.5)
    iq = iq_ref[0] * (IDX_DIM ** -0.5)
    iqh = [iq[:, h * IDX_DIM:(h + 1) * IDX_DIM].astype(BF16) for h in range(IDX_HEADS)]

    def score_chunk(c, carry):
        ik = ikbf_ref[0, _chunk_rows(c), :]
        score = jnp.zeros((KEY_CHUNK, qb), F32)
        for h in range(IDX_HEADS):
            score = score + iwt[h:h + 1, :] * jnp.maximum(_dot_nt(ik, iqh[h]), 0.0)
        key_scr[_chunk_rows(c), :] = _score_keys(jnp.where(row + c * KEY_CHUNK <= qcol, score, -jnp.inf))
        return carry

    lax.fori_loop(0, n_chunks, score_chunk, 0)
    thr, jstar = _topk_threshold(key_scr, half_scr, n_chunks, k_sel, t - 1)

    def bias_chunk(c, carry):
        krow = row + c * KEY_CHUNK
        sel = _selected(key_scr[_chunk_rows(c), :], krow, thr, jstar)
        bias_scr[_chunk_rows(c), :] = jnp.where(krow <= qcol, jnp.where(sel, 0.0, -jnp.inf), -jnp.inf)
        return carry

    lax.fori_loop(0, n_chunks, bias_chunk, 0)

    qs = q_ref[0] * (A_HEAD_DIM ** -0.5)
    lane = lax.broadcasted_iota(I32, (qb, LANES), 1)
    for g in range(A_WIDTH // LANES):
        qg = qs[:, g * LANES:(g + 1) * LANES]
        qpair_scr[g, :qb, :] = jnp.where(lane < A_HEAD_DIM, qg, 0.0).astype(BF16)
        qpair_scr[g, qb:, :] = jnp.where(lane < A_HEAD_DIM, 0.0, qg).astype(BF16)
    m_scr[...] = jnp.full(m_scr.shape, -jnp.inf, F32)
    l_scr[...] = jnp.zeros(l_scr.shape, F32)
    acc_scr[...] = jnp.zeros(acc_scr.shape, F32)

    def attend_chunk(c, carry):
        rows = _chunk_rows(c)
        bias = bias_scr[rows, :]
        for g in range(A_WIDTH // LANES):
            s_pair = _dot_nt(kbf_ref[0, rows, g * LANES:(g + 1) * LANES], qpair_scr[g])
            for half in range(2):
                h = 2 * g + half
                s = s_pair[:, half * qb:(half + 1) * qb] + bias
                m_old = m_scr[h:h + 1, :]
                m_new = jnp.maximum(m_old, jnp.max(_rows_reduce(s, jnp.maximum), axis=0, keepdims=True))
                m_use = jnp.where(m_new == -jnp.inf, 0.0, m_new)
                p = jnp.exp(s - m_use)
                resc = jnp.exp(m_old - m_use)
                m_scr[h:h + 1, :] = m_new
                l_scr[h:h + 1, :] = resc * l_scr[h:h + 1, :] + jnp.sum(_rows_reduce(p, jnp.add), axis=0, keepdims=True)
                hd = slice(h * A_HEAD_DIM, (h + 1) * A_HEAD_DIM)
                acc_scr[hd, :] = resc * acc_scr[hd, :] + _dot(vtbf_ref[0, hd, rows], p)
        return carry

    lax.fori_loop(0, n_chunks, attend_chunk, 0)
    out_row = lax.broadcasted_iota(I32, (LANES, qb), 0)
    for g in range(A_WIDTH // LANES):
        l_pair = jnp.where(out_row < A_HEAD_DIM, l_scr[2 * g:2 * g + 1, :], l_scr[2 * g + 1:2 * g + 2, :])
        o_ref[0, :, g * LANES:(g + 1) * LANES] = (acc_scr[g * LANES:(g + 1) * LANES, :] / l_pair).T


def _dsa_prompt(q, iq, ikwt, ikbf, kbf, vtbf, k_sel, qb):
    b, t, _ = q.shape
    blk = lambda w_: pl.BlockSpec((1, qb, w_), lambda i, j: (i, j, 0))
    full = lambda w_: pl.BlockSpec((1, t, w_), lambda i, j: (i, 0, 0))
    return pl.pallas_call(
        functools.partial(_dsa_prompt_kernel, k_sel), grid=(b, t // qb),
        in_specs=[blk(A_WIDTH), blk(IDX_HEADS * IDX_DIM),
                  pl.BlockSpec((1, SUBLANES, qb), lambda i, j: (i, IDX_DIM // SUBLANES, j)),
                  full(IDX_DIM), full(A_WIDTH), pl.BlockSpec((1, A_WIDTH, t), lambda i, j: (i, 0, 0))],
        out_specs=blk(A_WIDTH), out_shape=jax.ShapeDtypeStruct((b, t, A_WIDTH), F32),
        scratch_shapes=[pltpu.VMEM((t, qb), I32), pltpu.VMEM((t, qb), jnp.int16), pltpu.VMEM((t, qb), F32),
                        pltpu.VMEM((A_WIDTH // LANES, 2 * qb, LANES), BF16), pltpu.VMEM((A_HEADS, qb), F32),
                        pltpu.VMEM((A_HEADS, qb), F32), pltpu.VMEM((A_WIDTH, qb), F32)],
        compiler_params=_cparams(("parallel", "arbitrary"), _mib(48)), name="dsa_prompt",
    )(q, iq, ikwt, ikbf, kbf, vtbf)


S5_SEG = 64
S5_CHUNK = S5_SEG * SUBLANES


def _s5_prompt_kernel(u_ref, bblk_ref, cblk_ref, d_ref, a_ref, h0_ref, y_ref, hout_ref,
                      h_scr, pw_scr, hin_scr, carry_scr):
    c = pl.program_id(1)
    a = a_ref[...]
    a_re1, a_im1 = a[:, :S5_FLAT], a[:, S5_FLAT:]

    @pl.when(c == 0)
    def _():
        carry_scr[...] = h0_ref[0]
        pr, pi = a_re1, a_im1
        for k in range(S5_SEG):
            pw_scr[k:k + 1, :S5_FLAT] = pr
            pw_scr[k:k + 1, S5_FLAT:] = pi
            pr, pi = pr * a_re1 - pi * a_im1, pr * a_im1 + pi * a_re1

    up = u_ref[0]
    h_scr[...] = _dot(up, bblk_ref[...])

    a_re = jnp.broadcast_to(a_re1, (SUBLANES, S5_FLAT))
    a_im = jnp.broadcast_to(a_im1, (SUBLANES, S5_FLAT))

    def scan_step(k, carry):
        hr, hi = carry
        rows = pl.ds(pl.multiple_of(k * SUBLANES, SUBLANES), SUBLANES)
        nhr = a_re * hr - a_im * hi + h_scr[rows, :S5_FLAT]
        nhi = a_re * hi + a_im * hr + h_scr[rows, S5_FLAT:]
        h_scr[rows, :S5_FLAT] = nhr
        h_scr[rows, S5_FLAT:] = nhi
        return nhr, nhi

    zero = jnp.zeros((SUBLANES, S5_FLAT), F32)
    end_re, end_im = lax.fori_loop(0, S5_SEG, scan_step, (zero, zero))

    p_last_re, p_last_im = pw_scr[S5_SEG - 1:S5_SEG, :S5_FLAT], pw_scr[S5_SEG - 1:S5_SEG, S5_FLAT:]
    cr, ci = carry_scr[:, :S5_FLAT], carry_scr[:, S5_FLAT:]
    for j in range(SUBLANES):
        hin_scr[j:j + 1, :S5_FLAT] = cr
        hin_scr[j:j + 1, S5_FLAT:] = ci
        cr, ci = (end_re[j:j + 1] + p_last_re * cr - p_last_im * ci,
                  end_im[j:j + 1] + p_last_re * ci + p_last_im * cr)
    carry_scr[:, :S5_FLAT] = cr
    carry_scr[:, S5_FLAT:] = ci
    hin_re, hin_im = hin_scr[:, :S5_FLAT], hin_scr[:, S5_FLAT:]

    def fix_step(k, _):
        rows = pl.ds(pl.multiple_of(k * SUBLANES, SUBLANES), SUBLANES)
        pr = jnp.broadcast_to(pw_scr[pl.ds(k, 1), :S5_FLAT], (SUBLANES, S5_FLAT))
        pi = jnp.broadcast_to(pw_scr[pl.ds(k, 1), S5_FLAT:], (SUBLANES, S5_FLAT))
        h_scr[rows, :S5_FLAT] = h_scr[rows, :S5_FLAT] + pr * hin_re - pi * hin_im
        h_scr[rows, S5_FLAT:] = h_scr[rows, S5_FLAT:] + pr * hin_im + pi * hin_re
        return 0

    lax.fori_loop(0, S5_SEG, fix_step, 0)
    y_ref[0] = _dot(h_scr[...], cblk_ref[...]) + d_ref[...] * up

    @pl.when(c == pl.num_programs(1) - 1)
    def _():
        hout_ref[0] = carry_scr[...]


def _s5_prompt(u, bblk, cblk, dvec, avec, h0):
    b, t, _ = u.shape
    blk = pl.BlockSpec((1, S5_CHUNK, S5_WIDTH), lambda i, j: (i, j, 0))
    st = pl.BlockSpec((1, 1, 2 * S5_FLAT), lambda i, j: (i, 0, 0))
    return pl.pallas_call(
        _s5_prompt_kernel, grid=(b, t // S5_CHUNK),
        in_specs=[blk, _const_spec((S5_WIDTH, 2 * S5_FLAT)), _const_spec((2 * S5_FLAT, S5_WIDTH)),
                  _const_spec((1, S5_WIDTH)), _const_spec((1, 2 * S5_FLAT)), st],
        out_specs=[blk, st],
        out_shape=[jax.ShapeDtypeStruct((b, t, S5_WIDTH), F32), jax.ShapeDtypeStruct((b, 1, 2 * S5_FLAT), F32)],
        scratch_shapes=[pltpu.VMEM((S5_CHUNK, 2 * S5_FLAT), F32),
                        pltpu.VMEM((S5_SEG, 2 * S5_FLAT), F32), pltpu.VMEM((SUBLANES, 2 * S5_FLAT), F32),
                        pltpu.VMEM((1, 2 * S5_FLAT), F32)],
        compiler_params=_cparams(("parallel", "arbitrary"), _mib(32)), name="s5_prompt",
    )(u, bblk, cblk, dvec, avec, h0)


def _sgate_prompt_kernel(zc_ref, g_ref, b_ref, w_ref, bias_ref, o_ref):
    z = jax.nn.gelu(zc_ref[...])
    u = z[:, :SG_WIDTH]
    v = _layer_norm(z[:, SG_WIDTH:], g_ref[...], b_ref[...])
    ri = lax.broadcasted_iota(I32, (SG_CHUNK, SG_CHUNK), 0)
    ci = lax.broadcasted_iota(I32, (SG_CHUNK, SG_CHUNK), 1)
    wm = [jnp.where(ci <= ri, w_ref[g], 0.0).astype(BF16) for g in range(SG_GROUPS)]
    low = lax.broadcasted_iota(I32, (SG_CHUNK, LANES), 1) < SG_GROUP_CH
    bias = bias_ref[...]
    for c in range(zc_ref.shape[0] // SG_CHUNK):
        rows = slice(c * SG_CHUNK, (c + 1) * SG_CHUNK)
        for pr in range(SG_WIDTH // LANES):
            cols = slice(pr * LANES, (pr + 1) * LANES)
            vp = v[rows, cols].astype(BF16)
            s = jnp.where(low, _dot(wm[2 * pr], vp), _dot(wm[2 * pr + 1], vp)) + bias[:, cols]
            o_ref[rows, cols] = u[rows, cols] * s


def _sgate_prompt(zc, g, b, w, bias, tm):
    n = zc.shape[0]
    return pl.pallas_call(
        _sgate_prompt_kernel, grid=(n // tm,),
        in_specs=[pl.BlockSpec((tm, 2 * SG_WIDTH), lambda i: (i, 0)), _const_spec((1, SG_WIDTH)),
                  _const_spec((1, SG_WIDTH)), _const_spec((SG_GROUPS, SG_CHUNK, SG_CHUNK)),
                  _const_spec((SG_CHUNK, SG_WIDTH))],
        out_specs=pl.BlockSpec((tm, SG_WIDTH), lambda i: (i, 0)),
        out_shape=jax.ShapeDtypeStruct((n, SG_WIDTH), F32),
        compiler_params=_cparams(("parallel",), _mib(24)), name="sgate_prompt",
    )(zc, g, b, w, bias)


def _merge_kernel(alpha, x_ref, att_ref, y5_ref, sg_ref, wa_ref, wga_ref, wgb_ref, wc_ref, wg_ref, bg_ref, wmix_ref,
                  g_ref, b_ref, o_ref):
    x = x_ref[...]
    gates = jax.nn.sigmoid(_dot(x, wg_ref[...]) + bg_ref[...])
    br_a = _dot(att_ref[...], wa_ref[...])
    y5 = jax.nn.gelu(y5_ref[...])
    br_b = _dot(y5, wga_ref[...]) * jax.nn.sigmoid(_dot(y5, wgb_ref[...]))
    br_c = _dot(sg_ref[...], wc_ref[...])
    merged = (gates[:, :D_MODEL] * br_a + gates[:, D_MODEL:2 * D_MODEL] * br_b + gates[:, 2 * D_MODEL:] * br_c)
    o_ref[...] = _layer_norm(alpha * x + _dot(merged, wmix_ref[...]), g_ref[...], b_ref[...])


def _merge(alpha, x, att, y5, sg, wa, wga, wgb, wc, wg, bg, wmix, g, b, tm):
    n = x.shape[0]
    row = lambda w_: pl.BlockSpec((tm, w_), lambda i: (i, 0))
    return pl.pallas_call(
        functools.partial(_merge_kernel, alpha), grid=(n // tm,),
        in_specs=[row(D_MODEL), row(A_WIDTH), row(S5_WIDTH), row(SG_WIDTH),
                  _const_spec((A_WIDTH, D_MODEL)), _const_spec((S5_WIDTH, D_MODEL)), _const_spec((S5_WIDTH, D_MODEL)),
                  _const_spec((SG_WIDTH, D_MODEL)), _const_spec((D_MODEL, N_BRANCH * D_MODEL)),
                  _const_spec((1, N_BRANCH * D_MODEL)), _const_spec((D_MODEL, D_MODEL)),
                  _const_spec((1, D_MODEL)), _const_spec((1, D_MODEL))],
        out_specs=row(D_MODEL), out_shape=jax.ShapeDtypeStruct((n, D_MODEL), F32),
        compiler_params=_cparams(("parallel",), _mib(48)), name="branch_merge",
    )(x, att, y5, sg, wa, wga, wgb, wc, wg, bg, wmix, g, b)


def _two_linear_kernel(x_ref, w1_ref, w2_ref, o1_ref, o2_ref):
    x = x_ref[...]
    o1_ref[...] = _dot(x, w1_ref[...])
    o2_ref[...] = _dot(x, w2_ref[...])


def _two_linear(x, w1, w2, tm):
    n, k = x.shape
    m = w1.shape[1]
    return pl.pallas_call(
        _two_linear_kernel, grid=(n // tm,),
        in_specs=[pl.BlockSpec((tm, k), lambda i: (i, 0)), _const_spec((k, m)), _const_spec((k, m))],
        out_specs=[pl.BlockSpec((tm, m), lambda i: (i, 0))] * 2,
        out_shape=[jax.ShapeDtypeStruct((n, m), F32)] * 2,
        compiler_params=_cparams(("parallel",), _mib(24)), name="mem_kv_proj",
    )(x, w1, w2)


def _mem_prompt_kernel(alpha, x_ref, mk_ref, mv_ref, wq_ref, wo_ref, g_ref, b_ref, o_ref):
    x = x_ref[0]
    q = _dot(x, wq_ref[...])
    mk = mk_ref[0].astype(BF16)
    mv = mv_ref[0].astype(BF16)
    outs = []
    for h in range(MEM_HEADS):
        sl = slice(h * MEM_HEAD_DIM, (h + 1) * MEM_HEAD_DIM)
        s = _dot_nt(q[:, sl], mk[:, sl]) * (MEM_HEAD_DIM ** -0.5)
        m = jnp.max(s, axis=1, keepdims=True)
        p = jnp.exp(s - m)
        l = jnp.sum(p, axis=1, keepdims=True)
        outs.append(_dot(p, mv[:, sl]) / l)
    o = jnp.concatenate(outs, axis=1)
    o_ref[0] = _layer_norm(alpha * x + _dot(o, wo_ref[...]), g_ref[...], b_ref[...])


def _mem_prompt(alpha, x, mk, mv, wq, wo, g, b, tm):
    bsz, t, _ = x.shape
    return pl.pallas_call(
        functools.partial(_mem_prompt_kernel, alpha), grid=(bsz, t // tm),
        in_specs=[pl.BlockSpec((1, tm, D_MODEL), lambda i, j: (i, j, 0)),
                  pl.BlockSpec((1, MEM_TOKENS, MEM_WIDTH), lambda i, j: (i, 0, 0)),
                  pl.BlockSpec((1, MEM_TOKENS, MEM_WIDTH), lambda i, j: (i, 0, 0)),
                  _const_spec((D_MODEL, MEM_WIDTH)), _const_spec((MEM_WIDTH, D_MODEL)),
                  _const_spec((1, D_MODEL)), _const_spec((1, D_MODEL))],
        out_specs=pl.BlockSpec((1, tm, D_MODEL), lambda i, j: (i, j, 0)),
        out_shape=jax.ShapeDtypeStruct((bsz, t, D_MODEL), F32),
        compiler_params=_cparams(("parallel", "parallel"), _mib(32)), name="mem_attend_prompt",
    )(x, mk, mv, wq, wo, g, b)


def _ffn_kernel(alpha, x_ref, w1_ref, w2_ref, g_ref, b_ref, o_ref):
    x = x_ref[...]
    hid = jnp.square(jnp.maximum(_dot(x, w1_ref[...]), 0.0))
    o_ref[...] = _layer_norm(alpha * x + _dot(hid, w2_ref[...]), g_ref[...], b_ref[...])


def _ffn(alpha, x, w1, w2, g, b, tm):
    n = x.shape[0]
    row = pl.BlockSpec((tm, D_MODEL), lambda i: (i, 0))
    return pl.pallas_call(
        functools.partial(_ffn_kernel, alpha), grid=(n // tm,),
        in_specs=[row, _const_spec((D_MODEL, D_FF)), _const_spec((D_FF, D_MODEL)),
                  _const_spec((1, D_MODEL)), _const_spec((1, D_MODEL))],
        out_specs=row, out_shape=jax.ShapeDtypeStruct((n, D_MODEL), F32),
        compiler_params=_cparams(("parallel",), _mib(56)), name="relu2_mlp",
    )(x, w1, w2, g, b)


def _sample_s5_sg_kernel(u_ref, zc_ref, h0_ref, bblk_ref, cblk_ref, d_ref, a_ref, g_ref, b_ref, wd_ref, bd_ref,
                         y_ref, h_ref, sg_ref, vsg_ref):
    u = u_ref[...]
    a = a_ref[...]
    a_re, a_im = a[:, :S5_FLAT], a[:, S5_FLAT:]
    h0 = h0_ref[...]
    h0_re, h0_im = h0[:, :S5_FLAT], h0[:, S5_FLAT:]
    bu = _dot(u, bblk_ref[...])
    h_re = bu[:, :S5_FLAT] + (a_re * h0_re - a_im * h0_im)
    h_im = bu[:, S5_FLAT:] + (a_re * h0_im + a_im * h0_re)
    h_ref[:, :S5_FLAT] = h_re
    h_ref[:, S5_FLAT:] = h_im
    y_ref[...] = _dot(h_ref[...], cblk_ref[...]) + d_ref[...] * u
    z = jax.nn.gelu(zc_ref[...])
    v = _layer_norm(z[:, SG_WIDTH:], g_ref[...], b_ref[...])
    vsg_ref[...] = v
    sg_ref[...] = z[:, :SG_WIDTH] * (wd_ref[...] * v + bd_ref[...])


def _sample_s5_sg(u, zc, h0, bblk, cblk, dvec, avec, g, b, wd, bd):
    n = u.shape[0]
    full = lambda a: _const_spec(a.shape)
    args = (u, zc, h0, bblk, cblk, dvec, avec, g, b, wd, bd)
    return pl.pallas_call(
        _sample_s5_sg_kernel, grid=(1,), in_specs=[full(a) for a in args],
        out_specs=[_const_spec((n, S5_WIDTH)), _const_spec((n, 2 * S5_FLAT)), _const_spec((n, SG_WIDTH)),
                   _const_spec((n, SG_WIDTH))],
        out_shape=[jax.ShapeDtypeStruct((n, S5_WIDTH), F32), jax.ShapeDtypeStruct((n, 2 * S5_FLAT), F32),
                   jax.ShapeDtypeStruct((n, SG_WIDTH), F32), jax.ShapeDtypeStruct((n, SG_WIDTH), F32)],
        compiler_params=_cparams(("arbitrary",), _mib(24)), name="sample_s5_sgate",
    )(*args)


def _sample_scores_kernel(n_pages, pt_ref, iq_ref, iw_ref, ikw_ref, *refs):
    page_refs, o_ref = refs[:n_pages], refs[n_pages]
    iq = iq_ref[0]
    iw = iw_ref[0] * (IDX_HEADS ** -0.5)
    for j in range(n_pages):
        dots = _dot(iq, page_refs[j][0, 0]) * (IDX_DIM ** -0.5)
        o_ref[0, :, j * PAGE_SIZE:(j + 1) * PAGE_SIZE] = jnp.sum(iw * jnp.maximum(dots, 0.0), axis=0, keepdims=True)
    own = jnp.sum(iq * ikw_ref[0][:, :IDX_DIM], axis=1, keepdims=True) * (IDX_DIM ** -0.5)
    own = jnp.sum(iw * jnp.maximum(own, 0.0), axis=0, keepdims=True)
    o_ref[0, :, n_pages * PAGE_SIZE:] = jnp.broadcast_to(own, (1, LANES))


def _sample_scores(page_table, iq, iw, ikw, cache_kidx_t, layer):
    n, n_pages = page_table.shape
    width = n_pages * PAGE_SIZE + LANES
    page = lambda j: pl.BlockSpec((1, 1, IDX_DIM, PAGE_SIZE), lambda i, pt: (layer, pt[i, j], 0, 0))
    grid_spec = pltpu.PrefetchScalarGridSpec(
        num_scalar_prefetch=1, grid=(n,),
        in_specs=[pl.BlockSpec((1, SUBLANES, IDX_DIM), lambda i, pt: (i, 0, 0)),
                  pl.BlockSpec((1, SUBLANES, 1), lambda i, pt: (i, 0, 0)),
                  pl.BlockSpec((1, 1, LANES), lambda i, pt: (i, 0, 0))] + [page(j) for j in range(n_pages)],
        out_specs=pl.BlockSpec((1, 1, width), lambda i, pt: (i, 0, 0)))
    return pl.pallas_call(
        functools.partial(_sample_scores_kernel, n_pages), grid_spec=grid_spec,
        out_shape=jax.ShapeDtypeStruct((n, 1, width), F32),
        compiler_params=_cparams(("arbitrary",), _mib(24)), name="sample_idx_scores",
    )(page_table, iq, iw, ikw, *([cache_kidx_t] * n_pages))


def _sample_select_kernel(k_sel, sct_ref, o_ref, key_scr, half_scr):
    past, r = key_scr.shape
    n_chunks = past // KEY_CHUNK
    key_scr[...] = _score_keys(sct_ref[:past, :])
    xkey = _score_keys(sct_ref[past:past + 1, :])
    thr, jstar = _topk_threshold(key_scr, half_scr, n_chunks, k_sel, past, xkey)
    row = lax.broadcasted_iota(I32, (KEY_CHUNK, r), 0)
    for c in range(n_chunks):
        rows = slice(c * KEY_CHUNK, (c + 1) * KEY_CHUNK)
        o_ref[rows, :] = jnp.where(_selected(key_scr[rows, :], row + c * KEY_CHUNK, thr, jstar), 1.0, 0.0)
    own = jnp.where(_selected(xkey, jnp.full((1, r), past, I32), thr, jstar), 1.0, 0.0)
    o_ref[past:, :] = jnp.broadcast_to(own, (o_ref.shape[0] - past, r))


def _sample_select(scores_t, k_sel):
    rows, n = scores_t.shape
    past = rows - LANES
    assert past % KEY_CHUNK == 0
    return pl.pallas_call(
        functools.partial(_sample_select_kernel, k_sel), grid=(1,),
        in_specs=[_const_spec(scores_t.shape)], out_specs=_const_spec(scores_t.shape),
        out_shape=jax.ShapeDtypeStruct(scores_t.shape, F32),
        scratch_shapes=[pltpu.VMEM((past, n), I32), pltpu.VMEM((past, n), jnp.int16)],
        compiler_params=_cparams(("arbitrary",), _mib(24)), name="sample_topk_mask",
    )(scores_t)


def _sample_attend_kernel(n_pages, pt_ref, q_ref, kn_ref, vn_ref, m_ref, *refs):
    k_refs, v_refs, o_ref = refs[:n_pages], refs[n_pages:2 * n_pages], refs[2 * n_pages]
    q8 = q_ref[0] * (A_HEAD_DIM ** -0.5)
    q8b = q8.astype(BF16)
    head_of_row = lax.broadcasted_iota(I32, (A_HEADS, PAGE_SIZE), 0)
    pages = []
    for j in range(n_pages):
        sj = jnp.zeros((A_HEADS, PAGE_SIZE), F32)
        for h in range(A_HEADS):
            sj = jnp.where(head_of_row == h, _dot(q8b, k_refs[j][0, 0, h]), sj)
        pages.append(jnp.where(m_ref[0, :, j * PAGE_SIZE:(j + 1) * PAGE_SIZE] > 0.5, sj, -jnp.inf))
    s = jnp.concatenate(pages, axis=1)
    s_own = jnp.sum(q8 * kn_ref[0], axis=1, keepdims=True)
    s_own = jnp.where(m_ref[0, :, n_pages * PAGE_SIZE:n_pages * PAGE_SIZE + 1] > 0.5, s_own, -jnp.inf)
    m = jnp.maximum(jnp.max(s, axis=1, keepdims=True), s_own)
    p = jnp.exp(s - m)
    p_own = jnp.exp(s_own - m)
    l = jnp.sum(p, axis=1, keepdims=True) + p_own
    acc = p_own * vn_ref[0]
    head_of_out = lax.broadcasted_iota(I32, (A_HEADS, A_HEAD_DIM), 0)
    for j in range(n_pages):
        pj = p[:, j * PAGE_SIZE:(j + 1) * PAGE_SIZE].astype(BF16)
        for h in range(A_HEADS):
            acc = acc + jnp.where(head_of_out == h, _dot_nt(pj, v_refs[j][0, 0, h]), 0.0)
    o_ref[0] = acc / l


def _sample_attend(page_table, q, k_new, v_new, mask, cache_k_t, cache_v_t, layer):
    n, n_pages = page_table.shape
    page = lambda j: pl.BlockSpec((1, 1, A_HEADS, A_HEAD_DIM, PAGE_SIZE), lambda i, pt: (layer, pt[i, j], 0, 0, 0))
    tok = pl.BlockSpec((1, A_HEADS, A_HEAD_DIM), lambda i, pt: (i, 0, 0))
    grid_spec = pltpu.PrefetchScalarGridSpec(
        num_scalar_prefetch=1, grid=(n,),
        in_specs=[tok, tok, tok, pl.BlockSpec((1, 1, mask.shape[-1]), lambda i, pt: (i, 0, 0))]
        + [page(j) for j in range(n_pages)] * 2,
        out_specs=tok)
    return pl.pallas_call(
        functools.partial(_sample_attend_kernel, n_pages), grid_spec=grid_spec,
        out_shape=jax.ShapeDtypeStruct((n, A_HEADS, A_HEAD_DIM), F32),
        compiler_params=_cparams(("arbitrary",), _mib(40)), name="sample_dsa_attend",
    )(page_table, q, k_new, v_new, mask, *([cache_k_t] * n_pages), *([cache_v_t] * n_pages))


def _linear_kernel(x_ref, w_ref, o_ref):
    o_ref[...] = _dot(x_ref[...], w_ref[...])


def _linear(x, w):
    n, m = x.shape[0], w.shape[1]
    return pl.pallas_call(
        _linear_kernel, grid=(1,), in_specs=[_const_spec(x.shape), _const_spec(w.shape)],
        out_specs=_const_spec((n, m)), out_shape=jax.ShapeDtypeStruct((n, m), F32),
        compiler_params=_cparams(("arbitrary",), _mib(24)), name="linear",
    )(x, w)


def _linear_res_ln_kernel(alpha, o_ref, w_ref, x_ref, g_ref, b_ref, y_ref):
    y_ref[...] = _layer_norm(alpha * x_ref[...] + _dot(o_ref[...], w_ref[...]), g_ref[...], b_ref[...])


def _linear_res_ln(alpha, o, w, x, g, b):
    args = (o, w, x, g, b)
    return pl.pallas_call(
        functools.partial(_linear_res_ln_kernel, alpha), grid=(1,), in_specs=[_const_spec(a.shape) for a in args],
        out_specs=_const_spec(x.shape), out_shape=jax.ShapeDtypeStruct(x.shape, F32),
        compiler_params=_cparams(("arbitrary",), _mib(24)), name="linear_residual_norm",
    )(*args)


def _mem_sample_kernel(bs, q_ref, mk_ref, mv_ref, o_ref):
    for i in range(bs):
        q = q_ref[i]
        s = jnp.sum(mk_ref[0, i] * q[None], axis=-1, keepdims=True) * (MEM_HEAD_DIM ** -0.5)
        m = jnp.max(s, axis=0, keepdims=True)
        p = jnp.exp(s - m)
        l = jnp.sum(p, axis=0)
        o_ref[i] = jnp.sum(p * mv_ref[0, i], axis=0) / l


def _mem_sample(q, cache_mk, cache_mv, layer, bs):
    n = q.shape[0]
    cache = pl.BlockSpec((1, bs, MEM_TOKENS, MEM_HEADS, MEM_HEAD_DIM), lambda i: (layer, i, 0, 0, 0))
    tok = pl.BlockSpec((bs, MEM_HEADS, MEM_HEAD_DIM), lambda i: (i, 0, 0))
    return pl.pallas_call(
        functools.partial(_mem_sample_kernel, bs), grid=(n // bs,),
        in_specs=[tok, cache, cache], out_specs=tok,
        out_shape=jax.ShapeDtypeStruct((n, MEM_HEADS, MEM_HEAD_DIM), F32),
        compiler_params=_cparams(("parallel",), _mib(40)), name="mem_attend_sample",
    )(q, cache_mk, cache_mv)


def _pack_w_in(w_in):
    lead = w_in.shape[:-1]
    pts = np.cumsum([A_WIDTH, A_WIDTH, A_WIDTH, IDX_HEADS * IDX_DIM, IDX_DIM, IDX_HEADS, S5_WIDTH]).tolist()
    q, k, v, iq, ik, iw, u5, zc = jnp.split(w_in, pts, axis=-1)
    pad = jnp.zeros(lead + (IKW_PAD,), w_in.dtype)
    return jnp.concatenate([q, k, v, iq, ik, iw, pad, u5, zc], axis=-1).astype(BF16)


def _rope_tables(pos):
    half = A_HEAD_DIM // 2
    inv_freq = ROPE_THETA ** (-jnp.arange(half, dtype=F32) / half)
    ang = pos.astype(F32)[:, None] * inv_freq[None, :]
    c, s = jnp.cos(ang), jnp.sin(ang)
    return jnp.tile(c, (1, LANES // half)), jnp.concatenate([-s, s] * (LANES // A_HEAD_DIM), axis=1)


def _s5_matrices(lam_re, lam_im, log_dt, b_re, b_im, c_re, c_im, d):
    dt = jnp.exp(log_dt)[:, None]
    mag = jnp.exp(dt * lam_re)
    ang = dt * lam_im
    a_re = mag * jnp.cos(ang)
    a_im = mag * jnp.sin(ang)
    den = jnp.square(lam_re) + jnp.square(lam_im)
    z_re = ((a_re - 1.0) * lam_re + a_im * lam_im) / den
    z_im = (a_im * lam_re - (a_re - 1.0) * lam_im) / den
    bb_re = z_re[..., None] * b_re - z_im[..., None] * b_im
    bb_im = z_re[..., None] * b_im + z_im[..., None] * b_re
    eye = jnp.eye(S5_GROUPS, dtype=F32)
    in_re = jnp.einsum("gph,gk->ghkp", bb_re, eye).reshape(S5_WIDTH, S5_FLAT)
    in_im = jnp.einsum("gph,gk->ghkp", bb_im, eye).reshape(S5_WIDTH, S5_FLAT)
    bblk = jnp.concatenate([in_re, in_im], axis=1).astype(BF16)
    out_re = jnp.einsum("ghp,gk->gpkh", c_re, eye).reshape(S5_FLAT, S5_WIDTH)
    out_im = jnp.einsum("ghp,gk->gpkh", -c_im, eye).reshape(S5_FLAT, S5_WIDTH)
    cblk = jnp.concatenate([out_re, out_im], axis=0).astype(BF16)
    avec = jnp.concatenate([a_re.reshape(1, S5_FLAT), a_im.reshape(1, S5_FLAT)], axis=1)
    return bblk, cblk, d.reshape(1, S5_WIDTH), avec


def _pick_tile(n, pref):
    t = min(n, pref)
    assert n % t == 0
    return t


def kernel(x_prompt, x_sample, cache_k, cache_v, cache_kidx, cache_mem_k, cache_mem_v, state_s5_re, state_s5_im,
           page_table, mem_prompt, w_in, w_a_out, s5_lambda_re, s5_lambda_im, s5_log_dt, s5_b_re, s5_b_im,
           s5_c_re, s5_c_im, s5_d, w_glu_a, w_glu_b, sg_ln_g, sg_ln_b, sg_w, sg_b, w_c_out, w_gate, b_gate,
           w_mix_out, ln1_g, ln1_b, w_mq, w_mk, w_mv, w_mo, ln2_g, ln2_b, w_ff1, w_ff2, ln3_g, ln3_b):
    depth = w_in.shape[0]
    bp, tp, _ = x_prompt.shape
    db, ds, _ = x_sample.shape
    assert ds == 1 and tp % S5_CHUNK == 0 and tp % SG_CHUNK == 0 and tp % KEY_CHUNK == 0
    n_pages = page_table.shape[1]
    past = n_pages * PAGE_SIZE
    alpha = (2 * depth) ** 0.25
    bf = lambda a: a.astype(BF16)
    row = lambda a, l: a[l][None, :]

    w_in_p = _pack_w_in(w_in)
    w_a_out_b, w_glu_a_b, w_glu_b_b, w_c_out_b = bf(w_a_out), bf(w_glu_a), bf(w_glu_b), bf(w_c_out)
    w_gate_b, w_mix_b = bf(w_gate), bf(w_mix_out)
    w_mq_b, w_mk_b, w_mv_b, w_mo_b = bf(w_mq), bf(w_mk), bf(w_mv), bf(w_mo)
    w_ff1_b, w_ff2_b = bf(w_ff1), bf(w_ff2)
    s5m = [_s5_matrices(s5_lambda_re[l], s5_lambda_im[l], s5_log_dt[l], s5_b_re[l], s5_b_im[l], s5_c_re[l],
                        s5_c_im[l], s5_d[l]) for l in range(depth)]

    n = bp * tp
    tm = _pick_tile(tp, 256)
    cos_p, sin_p = _rope_tables(jnp.arange(tp))
    k_sel_p = min(TOPK_MAX, tp // 4)
    qb = _pick_tile(tp, 512)
    sg_bias = [jnp.repeat(sg_b[l].T, SG_GROUP_CH, axis=1) for l in range(depth)]
    mem2d = mem_prompt.reshape(bp * MEM_TOKENS, D_MODEL)
    h0 = jnp.zeros((bp, 1, 2 * S5_FLAT), F32)

    h = x_prompt.reshape(n, D_MODEL)
    ktp, vtp, iktp, mkp, mvp, s5p = [], [], [], [], [], []
    for l in range(depth):
        bblk, cblk, dvec, avec = s5m[l]
        q, iq, u5, zc, kbf, ikbf, kt, vt, vtbf, ikwt = _proj(h, w_in_p[l], cos_p, sin_p, tp, tm, False)
        r3 = lambda a: a.reshape(bp, tp, a.shape[-1])
        att = _dsa_prompt(r3(q), r3(iq), ikwt, r3(ikbf), r3(kbf), vtbf, k_sel_p, qb)
        u5p = u5.reshape(bp, tp // S5_CHUNK, SUBLANES, S5_SEG, S5_WIDTH).swapaxes(2, 3).reshape(bp, tp, S5_WIDTH)
        y5p, hfin = _s5_prompt(u5p, bblk, cblk, dvec, avec, h0)
        y5 = y5p.reshape(bp, tp // S5_CHUNK, S5_SEG, SUBLANES, S5_WIDTH).swapaxes(2, 3)
        sg = _sgate_prompt(zc, row(sg_ln_g, l), row(sg_ln_b, l), sg_w[l], sg_bias[l], _pick_tile(tp, 512))
        h = _merge(alpha, h, att.reshape(n, A_WIDTH), y5.reshape(n, S5_WIDTH), sg, w_a_out_b[l], w_glu_a_b[l],
                   w_glu_b_b[l], w_c_out_b[l], w_gate_b[l], row(b_gate, l), w_mix_b[l], row(ln1_g, l), row(ln1_b, l), tm)
        mk, mv = _two_linear(mem2d, w_mk_b[l], w_mv_b[l], _pick_tile(bp * MEM_TOKENS, 256))
        h = _mem_prompt(alpha, h.reshape(bp, tp, D_MODEL), mk.reshape(bp, MEM_TOKENS, MEM_WIDTH),
                        mv.reshape(bp, MEM_TOKENS, MEM_WIDTH), w_mq_b[l], w_mo_b[l], row(ln2_g, l), row(ln2_b, l),
                        tm).reshape(n, D_MODEL)
        h = _ffn(alpha, h, w_ff1_b[l], w_ff2_b[l], row(ln3_g, l), row(ln3_b, l), tm)
        ktp.append(kt)
        vtp.append(vt)
        iktp.append(ikwt[:, :IDX_DIM, :])
        mkp.append(mk.reshape(bp, MEM_TOKENS, MEM_HEADS, MEM_HEAD_DIM))
        mvp.append(mv.reshape(bp, MEM_TOKENS, MEM_HEADS, MEM_HEAD_DIM))
        s5p.append(hfin.reshape(bp, 2, S5_GROUPS, S5_STATE))
    y_prompt = h.reshape(bp, tp, D_MODEL)

    cache_k_t = jnp.transpose(cache_k, (0, 1, 3, 4, 2))
    cache_v_t = jnp.transpose(cache_v, (0, 1, 3, 4, 2))
    cache_kidx_t = jnp.swapaxes(cache_kidx, 2, 3)
    cos_s, sin_s = _rope_tables(jnp.full((db,), past, jnp.int32))
    k_sel_s = min(TOPK_MAX, (past + ds) // 4)
    h = x_sample.reshape(db, D_MODEL)
    kts, vts, ikts, s5s, sgs = [], [], [], [], []
    for l in range(depth):
        bblk, cblk, dvec, avec = s5m[l]
        q, iq, u5, zc, _, _, kt, vt, _, ikwt, k, v, ikw = _proj(h, w_in_p[l], cos_s, sin_s, db, db, True)
        head_pad = ((0, 0), (0, SUBLANES - IDX_HEADS), (0, 0))
        scores = _sample_scores(page_table, jnp.pad(iq.reshape(db, IDX_HEADS, IDX_DIM), head_pad),
                                jnp.pad(ikw[:, IDX_DIM:IDX_DIM + IDX_HEADS].reshape(db, IDX_HEADS, 1), head_pad),
                                ikw.reshape(db, 1, LANES), cache_kidx_t, l)
        mask = _sample_select(scores.reshape(db, past + LANES).T, k_sel_s).T
        hd = lambda a: a.reshape(db, A_HEADS, A_HEAD_DIM)
        att = _sample_attend(page_table, hd(q), hd(k), hd(v), mask.reshape(db, 1, past + LANES), cache_k_t, cache_v_t, l)
        h0s = jnp.concatenate([state_s5_re[l].reshape(db, S5_FLAT), state_s5_im[l].reshape(db, S5_FLAT)], axis=1)
        wd = jnp.repeat(sg_w[l][:, 0, 0], SG_GROUP_CH)[None, :]
        bd = jnp.repeat(sg_b[l][:, 0], SG_GROUP_CH)[None, :]
        y5, hnew, sg, vsg = _sample_s5_sg(u5, zc, h0s, bblk, cblk, dvec, avec, row(sg_ln_g, l), row(sg_ln_b, l), wd, bd)
        h = _merge(alpha, h, att.reshape(db, A_WIDTH), y5, sg, w_a_out_b[l], w_glu_a_b[l], w_glu_b_b[l], w_c_out_b[l],
                   w_gate_b[l], row(b_gate, l), w_mix_b[l], row(ln1_g, l), row(ln1_b, l), db)
        mq = _linear(h, w_mq_b[l])
        mo = _mem_sample(mq.reshape(db, MEM_HEADS, MEM_HEAD_DIM), cache_mem_k, cache_mem_v, l, 4)
        h = _linear_res_ln(alpha, mo.reshape(db, MEM_WIDTH), w_mo_b[l], h, row(ln2_g, l), row(ln2_b, l))
        h = _ffn(alpha, h, w_ff1_b[l], w_ff2_b[l], row(ln3_g, l), row(ln3_b, l), db)
        kts.append(kt)
        vts.append(vt)
        ikts.append(ikwt[:, :IDX_DIM, :])
        s5s.append(hnew.reshape(db, 2, S5_GROUPS, S5_STATE))
        sgs.append(vsg.reshape(db, ds, SG_WIDTH))
    y_sample = h.reshape(db, ds, D_MODEL)

    def heads_last(stack, batch, seq):
        return jnp.transpose(stack.reshape(depth, batch, A_HEADS, A_HEAD_DIM, seq), (0, 1, 4, 2, 3))

    s5p, s5s = jnp.stack(s5p), jnp.stack(s5s)
    return (y_prompt, y_sample,
            heads_last(jnp.stack(ktp), bp, tp), heads_last(jnp.stack(vtp), bp, tp), jnp.swapaxes(jnp.stack(iktp), 2, 3),
            jnp.stack(mkp), jnp.stack(mvp), s5p[:, :, 0], s5p[:, :, 1],
            heads_last(jnp.stack(kts), 1, db).reshape(depth, db, ds, A_HEADS, A_HEAD_DIM),
            heads_last(jnp.stack(vts), 1, db).reshape(depth, db, ds, A_HEADS, A_HEAD_DIM),
            jnp.swapaxes(jnp.stack(ikts), 2, 3).reshape(depth, db, ds, IDX_DIM),
            s5s[:, :, 0], s5s[:, :, 1], jnp.stack(sgs))
```

```python
import functools
import math

import numpy as np
import jax
import jax.numpy as jnp
from jax import lax
from jax.experimental import pallas as pl
from jax.experimental.pallas import tpu as pltpu

F32 = jnp.float32
BF16 = jnp.bfloat16
I32 = jnp.int32

D_MODEL = 1024
A_HEADS = 8
A_HEAD_DIM = 64
A_WIDTH = A_HEADS * A_HEAD_DIM
IDX_HEADS = 4
IDX_DIM = 64
TOPK_MAX = 256
ROPE_THETA = 10000.0
S5_GROUP_CH = 16
S5_GROUPS = 16
S5_WIDTH = S5_GROUPS * S5_GROUP_CH
S5_STATE = 64
S5_FLAT = S5_GROUPS * S5_STATE
SG_CHUNK = 128
SG_GROUPS = 4
SG_WIDTH = 256
SG_GROUP_CH = SG_WIDTH // SG_GROUPS
MEM_TOKENS = 256
MEM_HEADS = 4
MEM_HEAD_DIM = 128
MEM_WIDTH = MEM_HEADS * MEM_HEAD_DIM
D_FF = 4 * D_MODEL
N_BRANCH = 3
LN_EPS = 1e-5
PAGE_SIZE = 128

LANES = 128
SUBLANES = 8
VMEM_PHYSICAL_BYTES = 64 * 1024 * 1024

IKW_PAD = LANES - IDX_DIM - IDX_HEADS
OFF_Q = 0
OFF_K = OFF_Q + A_WIDTH
OFF_V = OFF_K + A_WIDTH
OFF_IQ = OFF_V + A_WIDTH
OFF_IKW = OFF_IQ + IDX_HEADS * IDX_DIM
OFF_U5 = OFF_IKW + LANES
OFF_ZC = OFF_U5 + S5_WIDTH
IN_PACKED = OFF_ZC + 2 * SG_WIDTH

INT_MIN = -2 ** 31
NEG_INF_KEY = int(np.array([0xFF800000 ^ 0x7FFFFFFF], dtype=np.uint32).view(np.int32)[0])


def _cparams(semantics, vmem_bytes):
    assert vmem_bytes < VMEM_PHYSICAL_BYTES
    return pltpu.CompilerParams(dimension_semantics=semantics, vmem_limit_bytes=int(vmem_bytes))


def _mib(n):
    return n * 1024 * 1024


def _const_spec(shape):
    n = len(shape)
    return pl.BlockSpec(shape, lambda *_: (0,) * n)


def _layer_norm(x, g, b):
    mu = jnp.mean(x, axis=-1, keepdims=True)
    xc = x - mu
    var = jnp.mean(xc * xc, axis=-1, keepdims=True)
    return xc * lax.rsqrt(var + LN_EPS) * g + b


def _dot(a, b):
    return jnp.dot(a.astype(BF16), b.astype(BF16), preferred_element_type=F32)


def _dot_nt(a, b):
    return lax.dot_general(a.astype(BF16), b.astype(BF16), (((1,), (1,)), ((), ())),
                           preferred_element_type=F32)


def _proj_kernel(emit_rows, x_ref, w_ref, cos_ref, sin_ref, q_ref, iq_ref, u5_ref, zc_ref, kbf_ref, ikbf_ref,
                 kt_ref, vt_ref, vtbf_ref, ikwt_ref, *row_refs):
    y = _dot(x_ref[...], w_ref[...])
    cos = cos_ref[...]
    sin = sin_ref[...]
    lane = lax.broadcasted_iota(I32, cos.shape, 1)
    first_half = (lane & (A_HEAD_DIM // 2)) == 0

    def rope(t, c, s):
        partner = jnp.where(first_half, pltpu.roll(t, LANES - A_HEAD_DIM // 2, 1), pltpu.roll(t, A_HEAD_DIM // 2, 1))
        return t * c + partner * s

    for g in range(A_WIDTH // LANES):
        sl = slice(g * LANES, (g + 1) * LANES)
        q_ref[:, sl] = rope(y[:, OFF_Q + g * LANES:OFF_Q + (g + 1) * LANES], cos, sin)
        kg = rope(y[:, OFF_K + g * LANES:OFF_K + (g + 1) * LANES], cos, sin)
        kbf_ref[:, sl] = kg.astype(BF16)
        kt_ref[0, sl, :] = kg.T
        vg = y[:, OFF_V + g * LANES:OFF_V + (g + 1) * LANES]
        vt = vg.T
        vt_ref[0, sl, :] = vt
        vtbf_ref[0, sl, :] = vt.astype(BF16)
        if emit_rows:
            row_refs[0][:, sl] = kg
            row_refs[1][:, sl] = vg
    for g in range(IDX_HEADS * IDX_DIM // LANES):
        iq_ref[:, g * LANES:(g + 1) * LANES] = rope(y[:, OFF_IQ + g * LANES:OFF_IQ + (g + 1) * LANES], cos, sin)
    is_key = lane < IDX_DIM
    ikw = rope(y[:, OFF_IKW:OFF_IKW + LANES], jnp.where(is_key, cos, 1.0), jnp.where(is_key, sin, 0.0))
    ikbf_ref[...] = ikw[:, :IDX_DIM].astype(BF16)
    ikwt_ref[0] = ikw.T
    if emit_rows:
        row_refs[2][...] = ikw
    u5_ref[...] = y[:, OFF_U5:OFF_U5 + S5_WIDTH]
    zc_ref[...] = y[:, OFF_ZC:OFF_ZC + 2 * SG_WIDTH]


def _proj(x, w, cos, sin, seq, tm, emit_rows):
    n = x.shape[0]
    nt = seq // tm
    row = lambda w_: pl.BlockSpec((tm, w_), lambda i: (i, 0))
    tab = pl.BlockSpec((tm, LANES), lambda i: (i % nt, 0))
    tr = lambda w_: pl.BlockSpec((1, w_, tm), lambda i: (i // nt, 0, i % nt))
    sds = jax.ShapeDtypeStruct
    row_out = [(A_WIDTH, F32), (IDX_HEADS * IDX_DIM, F32), (S5_WIDTH, F32), (2 * SG_WIDTH, F32), (A_WIDTH, BF16),
               (IDX_DIM, BF16)]
    tr_out = [(A_WIDTH, F32), (A_WIDTH, F32), (A_WIDTH, BF16), (LANES, F32)]
    extra = [(A_WIDTH, F32), (A_WIDTH, F32), (LANES, F32)] if emit_rows else []
    out_shape = ([sds((n, w_), dt) for w_, dt in row_out] + [sds((n // seq, w_, seq), dt) for w_, dt in tr_out]
                 + [sds((n, w_), dt) for w_, dt in extra])
    out_specs = [row(w_) for w_, _ in row_out] + [tr(w_) for w_, _ in tr_out] + [row(w_) for w_, _ in extra]
    return pl.pallas_call(
        functools.partial(_proj_kernel, emit_rows), grid=(n // tm,),
        in_specs=[row(D_MODEL), _const_spec((D_MODEL, IN_PACKED)), tab, tab],
        out_specs=out_specs, out_shape=out_shape,
        compiler_params=_cparams(("parallel",), _mib(40)), name="in_proj_rope",
    )(x, w, cos, sin)


KEY_CHUNK = 512


def _score_keys(score):
    score = jnp.where(score == 0.0, 0.0, score)
    bits = lax.bitcast_convert_type(score, I32)
    return jnp.where(bits < 0, bits ^ 0x7FFFFFFF, bits)


def _rows_reduce(v, op):
    accs = [None] * 4
    for i in range(v.shape[0] // SUBLANES):
        part = v[i * SUBLANES:(i + 1) * SUBLANES]
        accs[i % 4] = part if accs[i % 4] is None else op(accs[i % 4], part)
    return op(op(accs[0], accs[1]), op(accs[2], accs[3]))


def _chunk_rows(c):
    return pl.ds(pl.multiple_of(c * KEY_CHUNK, KEY_CHUNK), KEY_CHUNK)


def _count(key_ref, n_chunks, pred, extra=None):
    r = key_ref.shape[1]
    tile = SUBLANES * (4 // key_ref.dtype.itemsize)

    def body(c, acc):
        ch = key_ref[_chunk_rows(c), :]
        accs = [acc, None, None, None]
        for i in range(KEY_CHUNK // tile):
            v = pred(ch[i * tile:(i + 1) * tile], c * KEY_CHUNK + i * tile)
            accs[i % 4] = v if accs[i % 4] is None else accs[i % 4] + v
        return (accs[0] + accs[1]) + (accs[2] + accs[3])

    acc = lax.fori_loop(0, n_chunks, body, jnp.zeros((tile, r), key_ref.dtype))
    cnt = jnp.sum(acc.astype(I32), axis=0, keepdims=True)
    return cnt if extra is None else cnt + extra


def _as_int(mask):
    return jnp.where(mask, 1, 0)


def _as_i16(mask):
    return jnp.where(mask, jnp.int16(1), jnp.int16(0))


HALF = 1 << 15


def _search16(half_ref, n_chunks, need, xhalf=None):
    r = half_ref.shape[1]

    def bit_body(i, prefix):
        cand = (prefix | jnp.left_shift(jnp.int32(1), 15 - i)) - HALF
        cand16 = cand.astype(jnp.int16)
        cnt = _count(half_ref, n_chunks, lambda ch, _: _as_i16(ch >= cand16),
                     None if xhalf is None else _as_int(xhalf >= cand))
        return jnp.where(cnt >= need, cand + HALF, prefix)

    return lax.fori_loop(0, 16, bit_body, jnp.zeros((1, r), I32)) - HALF


def _topk_threshold(key_ref, half_ref, n_chunks, k_sel, last_row, xkey=None):
    r = key_ref.shape[1]

    def split(k):
        return jnp.right_shift(k, 16), (k & 0xFFFF) - HALF

    def fill_hi(c, carry):
        half_ref[_chunk_rows(c), :] = split(key_ref[_chunk_rows(c), :])[0].astype(jnp.int16)
        return carry

    lax.fori_loop(0, n_chunks, fill_hi, 0)
    xhi, xlo = (None, None) if xkey is None else split(xkey)
    t_hi = _search16(half_ref, n_chunks, k_sel, xhi)
    t_hi16 = t_hi.astype(jnp.int16)
    above = _count(half_ref, n_chunks, lambda ch, _: _as_i16(ch > t_hi16), None if xkey is None else _as_int(xhi > t_hi))

    def fill_lo(c, carry):
        hi, lo = split(key_ref[_chunk_rows(c), :])
        half_ref[_chunk_rows(c), :] = jnp.where(hi == t_hi, lo, -HALF).astype(jnp.int16)
        return carry

    lax.fori_loop(0, n_chunks, fill_lo, 0)
    t_lo = _search16(half_ref, n_chunks, k_sel - above, None if xkey is None else jnp.where(xhi == t_hi, xlo, -HALF))
    thr = t_hi * (2 * HALF) + (t_lo + HALF)

    cnt_gt = _count(key_ref, n_chunks, lambda ch, _: _as_int(ch > thr), None if xkey is None else _as_int(xkey > thr))
    cnt_eq = _count(key_ref, n_chunks, lambda ch, _: _as_int(ch == thr), None if xkey is None else _as_int(xkey == thr))
    need = k_sel - cnt_gt
    surplus = jnp.where(cnt_eq > need, _as_int(thr > NEG_INF_KEY), 0)
    any_surplus = jnp.max(surplus) > 0
    row = lax.broadcasted_iota(I32, (SUBLANES, r), 0)
    n_bits = int(last_row).bit_length()

    def tie_search():
        def body(i, lo):
            cand = lo + jnp.left_shift(jnp.int32(1), n_bits - 1 - i)
            extra = None if xkey is None else jnp.where(xkey == thr, _as_int(cand >= last_row), 0)
            cnt = _count(key_ref, n_chunks, lambda ch, r0: jnp.where(ch == thr, _as_int(row + r0 <= cand), 0), extra)
            return jnp.where(cnt < need, cand, lo)

        return lax.fori_loop(0, n_bits, body, jnp.full((1, r), -1, I32)) + 1

    jstar = lax.cond(any_surplus, tie_search, lambda: jnp.full((1, r), last_row, I32))
    return thr, jstar


def _selected(ch, row, thr, jstar):
    return jnp.where(ch == thr, _as_int(row <= jstar), _as_int(ch > thr)) > 0


def _dsa_prompt_kernel(k_sel, q_ref, iq_ref, iwt_ref, ikbf_ref, kbf_ref, vtbf_ref, o_ref, key_scr, half_scr, bias_scr,
                       qpair_scr, m_scr, l_scr, acc_scr):
    qb = q_ref.shape[1]
    t = kbf_ref.shape[1]
    q0 = pl.program_id(1) * qb
    n_chunks = (q0 + qb + KEY_CHUNK - 1) // KEY_CHUNK
    qcol = q0 + lax.broadcasted_iota(I32, (1, qb), 1)
    row = lax.broadcasted_iota(I32, (KEY_CHUNK, qb), 0)
    iwt = iwt_ref[0][:IDX_HEADS] * (IDX_HEADS ** -0.5)
    iq = iq_ref[0] * (IDX_DIM ** -0.5)
    iqh = [iq[:, h * IDX_DIM:(h + 1) * IDX_DIM].astype(BF16) for h in range(IDX_HEADS)]

    def score_chunk(c, carry):
        ik = ikbf_ref[0, _chunk_rows(c), :]
        score = jnp.zeros((KEY_CHUNK, qb), F32)
        for h in range(IDX_HEADS):
            score = score + iwt[h:h + 1, :] * jnp.maximum(_dot_nt(ik, iqh[h]), 0.0)
        key_scr[_chunk_rows(c), :] = _score_keys(jnp.where(row + c * KEY_CHUNK <= qcol, score, -jnp.inf))
        return carry

    lax.fori_loop(0, n_chunks, score_chunk, 0)
    thr, jstar = _topk_threshold(key_scr, half_scr, n_chunks, k_sel, t - 1)

    def bias_chunk(c, carry):
        krow = row + c * KEY_CHUNK
        sel = _selected(key_scr[_chunk_rows(c), :], krow, thr, jstar)
        bias_scr[_chunk_rows(c), :] = jnp.where(krow <= qcol, jnp.where(sel, 0.0, -jnp.inf), -jnp.inf)
        return carry

    lax.fori_loop(0, n_chunks, bias_chunk, 0)

    qs = q_ref[0] * (A_HEAD_DIM ** -0.5)
    lane = lax.broadcasted_iota(I32, (qb, LANES), 1)
    for g in range(A_WIDTH // LANES):
        qg = qs[:, g * LANES:(g + 1) * LANES]
        qpair_scr[g, :qb, :] = jnp.where(lane < A_HEAD_DIM, qg, 0.0).astype(BF16)
        qpair_scr[g, qb:, :] = jnp.where(lane < A_HEAD_DIM, 0.0, qg).astype(BF16)
    m_scr[...] = jnp.full(m_scr.shape, -jnp.inf, F32)
    l_scr[...] = jnp.zeros(l_scr.shape, F32)
    acc_scr[...] = jnp.zeros(acc_scr.shape, F32)

    def attend_chunk(c, carry):
        rows = _chunk_rows(c)
        bias = bias_scr[rows, :]
        for g in range(A_WIDTH // LANES):
            s_pair = _dot_nt(kbf_ref[0, rows, g * LANES:(g + 1) * LANES], qpair_scr[g])
            for half in range(2):
                h = 2 * g + half
                s = s_pair[:, half * qb:(half + 1) * qb] + bias
                m_old = m_scr[h:h + 1, :]
                m_new = jnp.maximum(m_old, jnp.max(_rows_reduce(s, jnp.maximum), axis=0, keepdims=True))
                m_use = jnp.where(m_new == -jnp.inf, 0.0, m_new)
                p = jnp.exp(s - m_use)
                resc = jnp.exp(m_old - m_use)
                m_scr[h:h + 1, :] = m_new
                l_scr[h:h + 1, :] = resc * l_scr[h:h + 1, :] + jnp.sum(_rows_reduce(p, jnp.add), axis=0, keepdims=True)
                hd = slice(h * A_HEAD_DIM, (h + 1) * A_HEAD_DIM)
                acc_scr[hd, :] = resc * acc_scr[hd, :] + _dot(vtbf_ref[0, hd, rows], p)
        return carry

    lax.fori_loop(0, n_chunks, attend_chunk, 0)
    out_row = lax.broadcasted_iota(I32, (LANES, qb), 0)
    for g in range(A_WIDTH // LANES):
        l_pair = jnp.where(out_row < A_HEAD_DIM, l_scr[2 * g:2 * g + 1, :], l_scr[2 * g + 1:2 * g + 2, :])
        o_ref[0, :, g * LANES:(g + 1) * LANES] = (acc_scr[g * LANES:(g + 1) * LANES, :] / l_pair).T


def _dsa_prompt(q, iq, ikwt, ikbf, kbf, vtbf, k_sel, qb):
    b, t, _ = q.shape
    blk = lambda w_: pl.BlockSpec((1, qb, w_), lambda i, j: (i, j, 0))
    full = lambda w_: pl.BlockSpec((1, t, w_), lambda i, j: (i, 0, 0))
    return pl.pallas_call(
        functools.partial(_dsa_prompt_kernel, k_sel), grid=(b, t // qb),
        in_specs=[blk(A_WIDTH), blk(IDX_HEADS * IDX_DIM),
                  pl.BlockSpec((1, SUBLANES, qb), lambda i, j: (i, IDX_DIM // SUBLANES, j)),
                  full(IDX_DIM), full(A_WIDTH), pl.BlockSpec((1, A_WIDTH, t), lambda i, j: (i, 0, 0))],
        out_specs=blk(A_WIDTH), out_shape=jax.ShapeDtypeStruct((b, t, A_WIDTH), F32),
        scratch_shapes=[pltpu.VMEM((t, qb), I32), pltpu.VMEM((t, qb), jnp.int16), pltpu.VMEM((t, qb), F32),
                        pltpu.VMEM((A_WIDTH // LANES, 2 * qb, LANES), BF16), pltpu.VMEM((A_HEADS, qb), F32),
                        pltpu.VMEM((A_HEADS, qb), F32), pltpu.VMEM((A_WIDTH, qb), F32)],
        compiler_params=_cparams(("parallel", "arbitrary"), _mib(48)), name="dsa_prompt",
    )(q, iq, ikwt, ikbf, kbf, vtbf)


S5_SEG = 64
S5_CHUNK = S5_SEG * SUBLANES


def _s5_prompt_kernel(u_ref, bblk_ref, cblk_ref, d_ref, a_ref, h0_ref, y_ref, hout_ref,
                      h_scr, pw_scr, hin_scr, carry_scr):
    c = pl.program_id(1)
    a = a_ref[...]
    a_re1, a_im1 = a[:, :S5_FLAT], a[:, S5_FLAT:]

    @pl.when(c == 0)
    def _():
        carry_scr[...] = h0_ref[0]
        pr, pi = a_re1, a_im1
        for k in range(S5_SEG):
            pw_scr[k:k + 1, :S5_FLAT] = pr
            pw_scr[k:k + 1, S5_FLAT:] = pi
            pr, pi = pr * a_re1 - pi * a_im1, pr * a_im1 + pi * a_re1

    up = u_ref[0]
    h_scr[...] = _dot(up, bblk_ref[...])

    a_re = jnp.broadcast_to(a_re1, (SUBLANES, S5_FLAT))
    a_im = jnp.broadcast_to(a_im1, (SUBLANES, S5_FLAT))

    def scan_step(k, carry):
        hr, hi = carry
        rows = pl.ds(pl.multiple_of(k * SUBLANES, SUBLANES), SUBLANES)
        nhr = a_re * hr - a_im * hi + h_scr[rows, :S5_FLAT]
        nhi = a_re * hi + a_im * hr + h_scr[rows, S5_FLAT:]
        h_scr[rows, :S5_FLAT] = nhr
        h_scr[rows, S5_FLAT:] = nhi
        return nhr, nhi

    zero = jnp.zeros((SUBLANES, S5_FLAT), F32)
    end_re, end_im = lax.fori_loop(0, S5_SEG, scan_step, (zero, zero))

    p_last_re, p_last_im = pw_scr[S5_SEG - 1:S5_SEG, :S5_FLAT], pw_scr[S5_SEG - 1:S5_SEG, S5_FLAT:]
    cr, ci = carry_scr[:, :S5_FLAT], carry_scr[:, S5_FLAT:]
    for j in range(SUBLANES):
        hin_scr[j:j + 1, :S5_FLAT] = cr
        hin_scr[j:j + 1, S5_FLAT:] = ci
        cr, ci = (end_re[j:j + 1] + p_last_re * cr - p_last_im * ci,
                  end_im[j:j + 1] + p_last_re * ci + p_last_im * cr)
    carry_scr[:, :S5_FLAT] = cr
    carry_scr[:, S5_FLAT:] = ci
    hin_re, hin_im = hin_scr[:, :S5_FLAT], hin_scr[:, S5_FLAT:]

    def fix_step(k, _):
        rows = pl.ds(pl.multiple_of(k * SUBLANES, SUBLANES), SUBLANES)
        pr = jnp.broadcast_to(pw_scr[pl.ds(k, 1), :S5_FLAT], (SUBLANES, S5_FLAT))
        pi = jnp.broadcast_to(pw_scr[pl.ds(k, 1), S5_FLAT:], (SUBLANES, S5_FLAT))
        h_scr[rows, :S5_FLAT] = h_scr[rows, :S5_FLAT] + pr * hin_re - pi * hin_im
        h_scr[rows, S5_FLAT:] = h_scr[rows, S5_FLAT:] + pr * hin_im + pi * hin_re
        return 0

    lax.fori_loop(0, S5_SEG, fix_step, 0)
    y_ref[0] = _dot(h_scr[...], cblk_ref[...]) + d_ref[...] * up

    @pl.when(c == pl.num_programs(1) - 1)
    def _():
        hout_ref[0] = carry_scr[...]


def _s5_prompt(u, bblk, cblk, dvec, avec, h0):
    b, t, _ = u.shape
    blk = pl.BlockSpec((1, S5_CHUNK, S5_WIDTH), lambda i, j: (i, j, 0))
    st = pl.BlockSpec((1, 1, 2 * S5_FLAT), lambda i, j: (i, 0, 0))
    return pl.pallas_call(
        _s5_prompt_kernel, grid=(b, t // S5_CHUNK),
        in_specs=[blk, _const_spec((S5_WIDTH, 2 * S5_FLAT)), _const_spec((2 * S5_FLAT, S5_WIDTH)),
                  _const_spec((1, S5_WIDTH)), _const_spec((1, 2 * S5_FLAT)), st],
        out_specs=[blk, st],
        out_shape=[jax.ShapeDtypeStruct((b, t, S5_WIDTH), F32), jax.ShapeDtypeStruct((b, 1, 2 * S5_FLAT), F32)],
        scratch_shapes=[pltpu.VMEM((S5_CHUNK, 2 * S5_FLAT), F32),
                        pltpu.VMEM((S5_SEG, 2 * S5_FLAT), F32), pltpu.VMEM((SUBLANES, 2 * S5_FLAT), F32),
                        pltpu.VMEM((1, 2 * S5_FLAT), F32)],
        compiler_params=_cparams(("parallel", "arbitrary"), _mib(32)), name="s5_prompt",
    )(u, bblk, cblk, dvec, avec, h0)


def _sgate_prompt_kernel(zc_ref, g_ref, b_ref, w_ref, bias_ref, o_ref):
    z = jax.nn.gelu(zc_ref[...])
    u = z[:, :SG_WIDTH]
    v = _layer_norm(z[:, SG_WIDTH:], g_ref[...], b_ref[...])
    ri = lax.broadcasted_iota(I32, (SG_CHUNK, SG_CHUNK), 0)
    ci = lax.broadcasted_iota(I32, (SG_CHUNK, SG_CHUNK), 1)
    wm = [jnp.where(ci <= ri, w_ref[g], 0.0).astype(BF16) for g in range(SG_GROUPS)]
    low = lax.broadcasted_iota(I32, (SG_CHUNK, LANES), 1) < SG_GROUP_CH
    bias = bias_ref[...]
    for c in range(zc_ref.shape[0] // SG_CHUNK):
        rows = slice(c * SG_CHUNK, (c + 1) * SG_CHUNK)
        for pr in range(SG_WIDTH // LANES):
            cols = slice(pr * LANES, (pr + 1) * LANES)
            vp = v[rows, cols].astype(BF16)
            s = jnp.where(low, _dot(wm[2 * pr], vp), _dot(wm[2 * pr + 1], vp)) + bias[:, cols]
            o_ref[rows, cols] = u[rows, cols] * s


def _sgate_prompt(zc, g, b, w, bias, tm):
    n = zc.shape[0]
    return pl.pallas_call(
        _sgate_prompt_kernel, grid=(n // tm,),
        in_specs=[pl.BlockSpec((tm, 2 * SG_WIDTH), lambda i: (i, 0)), _const_spec((1, SG_WIDTH)),
                  _const_spec((1, SG_WIDTH)), _const_spec((SG_GROUPS, SG_CHUNK, SG_CHUNK)),
                  _const_spec((SG_CHUNK, SG_WIDTH))],
        out_specs=pl.BlockSpec((tm, SG_WIDTH), lambda i: (i, 0)),
        out_shape=jax.ShapeDtypeStruct((n, SG_WIDTH), F32),
        compiler_params=_cparams(("parallel",), _mib(24)), name="sgate_prompt",
    )(zc, g, b, w, bias)


def _merge_kernel(alpha, x_ref, att_ref, y5_ref, sg_ref, wa_ref, wga_ref, wgb_ref, wc_ref, wg_ref, bg_ref, wmix_ref,
                  g_ref, b_ref, o_ref):
    x = x_ref[...]
    gates = jax.nn.sigmoid(_dot(x, wg_ref[...]) + bg_ref[...])
    br_a = _dot(att_ref[...], wa_ref[...])
    y5 = jax.nn.gelu(y5_ref[...])
    br_b = _dot(y5, wga_ref[...]) * jax.nn.sigmoid(_dot(y5, wgb_ref[...]))
    br_c = _dot(sg_ref[...], wc_ref[...])
    merged = (gates[:, :D_MODEL] * br_a + gates[:, D_MODEL:2 * D_MODEL] * br_b + gates[:, 2 * D_MODEL:] * br_c)
    o_ref[...] = _layer_norm(alpha * x + _dot(merged, wmix_ref[...]), g_ref[...], b_ref[...])


def _merge(alpha, x, att, y5, sg, wa, wga, wgb, wc, wg, bg, wmix, g, b, tm):
    n = x.shape[0]
    row = lambda w_: pl.BlockSpec((tm, w_), lambda i: (i, 0))
    return pl.pallas_call(
        functools.partial(_merge_kernel, alpha), grid=(n // tm,),
        in_specs=[row(D_MODEL), row(A_WIDTH), row(S5_WIDTH), row(SG_WIDTH),
                  _const_spec((A_WIDTH, D_MODEL)), _const_spec((S5_WIDTH, D_MODEL)), _const_spec((S5_WIDTH, D_MODEL)),
                  _const_spec((SG_WIDTH, D_MODEL)), _const_spec((D_MODEL, N_BRANCH * D_MODEL)),
                  _const_spec((1, N_BRANCH * D_MODEL)), _const_spec((D_MODEL, D_MODEL)),
                  _const_spec((1, D_MODEL)), _const_spec((1, D_MODEL))],
        out_specs=row(D_MODEL), out_shape=jax.ShapeDtypeStruct((n, D_MODEL), F32),
        compiler_params=_cparams(("parallel",), _mib(48)), name="branch_merge",
    )(x, att, y5, sg, wa, wga, wgb, wc, wg, bg, wmix, g, b)


def _two_linear_kernel(x_ref, w1_ref, w2_ref, o1_ref, o2_ref):
    x = x_ref[...]
    o1_ref[...] = _dot(x, w1_ref[...])
    o2_ref[...] = _dot(x, w2_ref[...])


def _two_linear(x, w1, w2, tm):
    n, k = x.shape
    m = w1.shape[1]
    return pl.pallas_call(
        _two_linear_kernel, grid=(n // tm,),
        in_specs=[pl.BlockSpec((tm, k), lambda i: (i, 0)), _const_spec((k, m)), _const_spec((k, m))],
        out_specs=[pl.BlockSpec((tm, m), lambda i: (i, 0))] * 2,
        out_shape=[jax.ShapeDtypeStruct((n, m), F32)] * 2,
        compiler_params=_cparams(("parallel",), _mib(24)), name="mem_kv_proj",
    )(x, w1, w2)


def _mem_prompt_kernel(alpha, x_ref, mk_ref, mv_ref, wq_ref, wo_ref, g_ref, b_ref, o_ref):
    x = x_ref[0]
    q = _dot(x, wq_ref[...])
    mk = mk_ref[0].astype(BF16)
    mv = mv_ref[0].astype(BF16)
    outs = []
    for h in range(MEM_HEADS):
        sl = slice(h * MEM_HEAD_DIM, (h + 1) * MEM_HEAD_DIM)
        s = _dot_nt(q[:, sl], mk[:, sl]) * (MEM_HEAD_DIM ** -0.5)
        m = jnp.max(s, axis=1, keepdims=True)
        p = jnp.exp(s - m)
        l = jnp.sum(p, axis=1, keepdims=True)
        outs.append(_dot(p, mv[:, sl]) / l)
    o = jnp.concatenate(outs, axis=1)
    o_ref[0] = _layer_norm(alpha * x + _dot(o, wo_ref[...]), g_ref[...], b_ref[...])


def _mem_prompt(alpha, x, mk, mv, wq, wo, g, b, tm):
    bsz, t, _ = x.shape
    return pl.pallas_call(
        functools.partial(_mem_prompt_kernel, alpha), grid=(bsz, t // tm),
        in_specs=[pl.BlockSpec((1, tm, D_MODEL), lambda i, j: (i, j, 0)),
                  pl.BlockSpec((1, MEM_TOKENS, MEM_WIDTH), lambda i, j: (i, 0, 0)),
                  pl.BlockSpec((1, MEM_TOKENS, MEM_WIDTH), lambda i, j: (i, 0, 0)),
                  _const_spec((D_MODEL, MEM_WIDTH)), _const_spec((MEM_WIDTH, D_MODEL)),
                  _const_spec((1, D_MODEL)), _const_spec((1, D_MODEL))],
        out_specs=pl.BlockSpec((1, tm, D_MODEL), lambda i, j: (i, j, 0)),
        out_shape=jax.ShapeDtypeStruct((bsz, t, D_MODEL), F32),
        compiler_params=_cparams(("parallel", "parallel"), _mib(32)), name="mem_attend_prompt",
    )(x, mk, mv, wq, wo, g, b)


def _ffn_kernel(alpha, x_ref, w1_ref, w2_ref, g_ref, b_ref, o_ref):
    x = x_ref[...]
    hid = jnp.square(jnp.maximum(_dot(x, w1_ref[...]), 0.0))
    o_ref[...] = _layer_norm(alpha * x + _dot(hid, w2_ref[...]), g_ref[...], b_ref[...])


def _ffn(alpha, x, w1, w2, g, b, tm):
    n = x.shape[0]
    row = pl.BlockSpec((tm, D_MODEL), lambda i: (i, 0))
    return pl.pallas_call(
        functools.partial(_ffn_kernel, alpha), grid=(n // tm,),
        in_specs=[row, _const_spec((D_MODEL, D_FF)), _const_spec((D_FF, D_MODEL)),
                  _const_spec((1, D_MODEL)), _const_spec((1, D_MODEL))],
        out_specs=row, out_shape=jax.ShapeDtypeStruct((n, D_MODEL), F32),
        compiler_params=_cparams(("parallel",), _mib(56)), name="relu2_mlp",
    )(x, w1, w2, g, b)


def _sample_s5_sg_kernel(u_ref, zc_ref, h0_ref, bblk_ref, cblk_ref, d_ref, a_ref, g_ref, b_ref, wd_ref, bd_ref,
                         y_ref, h_ref, sg_ref, vsg_ref):
    u = u_ref[...]
    a = a_ref[...]
    a_re, a_im = a[:, :S5_FLAT], a[:, S5_FLAT:]
    h0 = h0_ref[...]
    h0_re, h0_im = h0[:, :S5_FLAT], h0[:, S5_FLAT:]
    bu = _dot(u, bblk_ref[...])
    h_re = bu[:, :S5_FLAT] + (a_re * h0_re - a_im * h0_im)
    h_im = bu[:, S5_FLAT:] + (a_re * h0_im + a_im * h0_re)
    h_ref[:, :S5_FLAT] = h_re
    h_ref[:, S5_FLAT:] = h_im
    y_ref[...] = _dot(h_ref[...], cblk_ref[...]) + d_ref[...] * u
    z = jax.nn.gelu(zc_ref[...])
    v = _layer_norm(z[:, SG_WIDTH:], g_ref[...], b_ref[...])
    vsg_ref[...] = v
    sg_ref[...] = z[:, :SG_WIDTH] * (wd_ref[...] * v + bd_ref[...])


def _sample_s5_sg(u, zc, h0, bblk, cblk, dvec, avec, g, b, wd, bd):
    n = u.shape[0]
    full = lambda a: _const_spec(a.shape)
    args = (u, zc, h0, bblk, cblk, dvec, avec, g, b, wd, bd)
    return pl.pallas_call(
        _sample_s5_sg_kernel, grid=(1,), in_specs=[full(a) for a in args],
        out_specs=[_const_spec((n, S5_WIDTH)), _const_spec((n, 2 * S5_FLAT)), _const_spec((n, SG_WIDTH)),
                   _const_spec((n, SG_WIDTH))],
        out_shape=[jax.ShapeDtypeStruct((n, S5_WIDTH), F32), jax.ShapeDtypeStruct((n, 2 * S5_FLAT), F32),
                   jax.ShapeDtypeStruct((n, SG_WIDTH), F32), jax.ShapeDtypeStruct((n, SG_WIDTH), F32)],
        compiler_params=_cparams(("arbitrary",), _mib(24)), name="sample_s5_sgate",
    )(*args)


def _sample_scores_kernel(n_pages, group, pt_ref, iq_ref, iw_ref, ikw_ref, *refs):
    page_refs, o_ref = refs[:group * n_pages], refs[group * n_pages]
    for g in range(group):
        iq = iq_ref[g]
        iw = iw_ref[g] * (IDX_HEADS ** -0.5)
        for j in range(n_pages):
            dots = _dot(iq, page_refs[g * n_pages + j][0, 0]) * (IDX_DIM ** -0.5)
            o_ref[g, :, j * PAGE_SIZE:(j + 1) * PAGE_SIZE] = jnp.sum(iw * jnp.maximum(dots, 0.0), axis=0, keepdims=True)
        own = jnp.sum(iq * ikw_ref[g][:, :IDX_DIM], axis=1, keepdims=True) * (IDX_DIM ** -0.5)
        own = jnp.sum(iw * jnp.maximum(own, 0.0), axis=0, keepdims=True)
        o_ref[g, :, n_pages * PAGE_SIZE:] = jnp.broadcast_to(own, (1, LANES))


def _sample_scores(page_table, iq, iw, ikw, cache_kidx_t, layer, group):
    n, n_pages = page_table.shape
    width = n_pages * PAGE_SIZE + LANES
    page = lambda g, j: pl.BlockSpec((1, 1, IDX_DIM, PAGE_SIZE), lambda i, pt: (layer, pt[i * group + g, j], 0, 0))
    grid_spec = pltpu.PrefetchScalarGridSpec(
        num_scalar_prefetch=1, grid=(n // group,),
        in_specs=[pl.BlockSpec((group, SUBLANES, IDX_DIM), lambda i, pt: (i, 0, 0)),
                  pl.BlockSpec((group, SUBLANES, 1), lambda i, pt: (i, 0, 0)),
                  pl.BlockSpec((group, 1, LANES), lambda i, pt: (i, 0, 0))]
        + [page(g, j) for g in range(group) for j in range(n_pages)],
        out_specs=pl.BlockSpec((group, 1, width), lambda i, pt: (i, 0, 0)))
    return pl.pallas_call(
        functools.partial(_sample_scores_kernel, n_pages, group), grid_spec=grid_spec,
        out_shape=jax.ShapeDtypeStruct((n, 1, width), F32),
        compiler_params=_cparams(("arbitrary",), _mib(24)), name="sample_idx_scores",
    )(page_table, iq, iw, ikw, *([cache_kidx_t] * (group * n_pages)))


def _sample_select_kernel(k_sel, sct_ref, o_ref, key_scr, half_scr):
    past, r = key_scr.shape
    n_chunks = past // KEY_CHUNK
    key_scr[...] = _score_keys(sct_ref[:past, :])
    xkey = _score_keys(sct_ref[past:past + 1, :])
    thr, jstar = _topk_threshold(key_scr, half_scr, n_chunks, k_sel, past, xkey)
    row = lax.broadcasted_iota(I32, (KEY_CHUNK, r), 0)
    for c in range(n_chunks):
        rows = slice(c * KEY_CHUNK, (c + 1) * KEY_CHUNK)
        o_ref[rows, :] = jnp.where(_selected(key_scr[rows, :], row + c * KEY_CHUNK, thr, jstar), 1.0, 0.0)
    own = jnp.where(_selected(xkey, jnp.full((1, r), past, I32), thr, jstar), 1.0, 0.0)
    o_ref[past:, :] = jnp.broadcast_to(own, (o_ref.shape[0] - past, r))


def _sample_select(scores_t, k_sel):
    rows, n = scores_t.shape
    past = rows - LANES
    assert past % KEY_CHUNK == 0
    return pl.pallas_call(
        functools.partial(_sample_select_kernel, k_sel), grid=(1,),
        in_specs=[_const_spec(scores_t.shape)], out_specs=_const_spec(scores_t.shape),
        out_shape=jax.ShapeDtypeStruct(scores_t.shape, F32),
        scratch_shapes=[pltpu.VMEM((past, n), I32), pltpu.VMEM((past, n), jnp.int16)],
        compiler_params=_cparams(("arbitrary",), _mib(24)), name="sample_topk_mask",
    )(scores_t)


def _sample_attend_kernel(n_pages, pt_ref, q_ref, kn_ref, vn_ref, m_ref, *refs):
    k_refs, v_refs, o_ref = refs[:n_pages], refs[n_pages:2 * n_pages], refs[2 * n_pages]
    q8 = q_ref[0] * (A_HEAD_DIM ** -0.5)
    q8b = q8.astype(BF16)
    head_of_row = lax.broadcasted_iota(I32, (A_HEADS, PAGE_SIZE), 0)
    pages = []
    for j in range(n_pages):
        sj = jnp.zeros((A_HEADS, PAGE_SIZE), F32)
        for h in range(A_HEADS):
            sj = jnp.where(head_of_row == h, _dot(q8b, k_refs[j][0, 0, h]), sj)
        pages.append(jnp.where(m_ref[0, :, j * PAGE_SIZE:(j + 1) * PAGE_SIZE] > 0.5, sj, -jnp.inf))
    s = jnp.concatenate(pages, axis=1)
    s_own = jnp.sum(q8 * kn_ref[0], axis=1, keepdims=True)
    s_own = jnp.where(m_ref[0, :, n_pages * PAGE_SIZE:n_pages * PAGE_SIZE + 1] > 0.5, s_own, -jnp.inf)
    m = jnp.maximum(jnp.max(s, axis=1, keepdims=True), s_own)
    p = jnp.exp(s - m)
    p_own = jnp.exp(s_own - m)
    l = jnp.sum(p, axis=1, keepdims=True) + p_own
    acc = p_own * vn_ref[0]
    head_of_out = lax.broadcasted_iota(I32, (A_HEADS, A_HEAD_DIM), 0)
    for j in range(n_pages):
        pj = p[:, j * PAGE_SIZE:(j + 1) * PAGE_SIZE].astype(BF16)
        for h in range(A_HEADS):
            acc = acc + jnp.where(head_of_out == h, _dot_nt(pj, v_refs[j][0, 0, h]), 0.0)
    o_ref[0] = acc / l


def _sample_attend(page_table, q, k_new, v_new, mask, cache_k_t, cache_v_t, layer):
    n, n_pages = page_table.shape
    page = lambda j: pl.BlockSpec((1, 1, A_HEADS, A_HEAD_DIM, PAGE_SIZE), lambda i, pt: (layer, pt[i, j], 0, 0, 0))
    tok = pl.BlockSpec((1, A_HEADS, A_HEAD_DIM), lambda i, pt: (i, 0, 0))
    grid_spec = pltpu.PrefetchScalarGridSpec(
        num_scalar_prefetch=1, grid=(n,),
        in_specs=[tok, tok, tok, pl.BlockSpec((1, 1, mask.shape[-1]), lambda i, pt: (i, 0, 0))]
        + [page(j) for j in range(n_pages)] * 2,
        out_specs=tok)
    return pl.pallas_call(
        functools.partial(_sample_attend_kernel, n_pages), grid_spec=grid_spec,
        out_shape=jax.ShapeDtypeStruct((n, A_HEADS, A_HEAD_DIM), F32),
        compiler_params=_cparams(("arbitrary",), _mib(40)), name="sample_dsa_attend",
    )(page_table, q, k_new, v_new, mask, *([cache_k_t] * n_pages), *([cache_v_t] * n_pages))


def _linear_kernel(x_ref, w_ref, o_ref):
    o_ref[...] = _dot(x_ref[...], w_ref[...])


def _linear(x, w):
    n, m = x.shape[0], w.shape[1]
    return pl.pallas_call(
        _linear_kernel, grid=(1,), in_specs=[_const_spec(x.shape), _const_spec(w.shape)],
        out_specs=_const_spec((n, m)), out_shape=jax.ShapeDtypeStruct((n, m), F32),
        compiler_params=_cparams(("arbitrary",), _mib(24)), name="linear",
    )(x, w)


def _linear_res_ln_kernel(alpha, o_ref, w_ref, x_ref, g_ref, b_ref, y_ref):
    y_ref[...] = _layer_norm(alpha * x_ref[...] + _dot(o_ref[...], w_ref[...]), g_ref[...], b_ref[...])


def _linear_res_ln(alpha, o, w, x, g, b):
    args = (o, w, x, g, b)
    return pl.pallas_call(
        functools.partial(_linear_res_ln_kernel, alpha), grid=(1,), in_specs=[_const_spec(a.shape) for a in args],
        out_specs=_const_spec(x.shape), out_shape=jax.ShapeDtypeStruct(x.shape, F32),
        compiler_params=_cparams(("arbitrary",), _mib(24)), name="linear_residual_norm",
    )(*args)


def _mem_sample_kernel(bs, q_ref, mk_ref, mv_ref, o_ref):
    for i in range(bs):
        q = q_ref[i]
        s = jnp.sum(mk_ref[0, i] * q[None], axis=-1, keepdims=True) * (MEM_HEAD_DIM ** -0.5)
        m = jnp.max(s, axis=0, keepdims=True)
        p = jnp.exp(s - m)
        l = jnp.sum(p, axis=0)
        o_ref[i] = jnp.sum(p * mv_ref[0, i], axis=0) / l


def _mem_sample(q, cache_mk, cache_mv, layer, bs):
    n = q.shape[0]
    cache = pl.BlockSpec((1, bs, MEM_TOKENS, MEM_HEADS, MEM_HEAD_DIM), lambda i: (layer, i, 0, 0, 0))
    tok = pl.BlockSpec((bs, MEM_HEADS, MEM_HEAD_DIM), lambda i: (i, 0, 0))
    return pl.pallas_call(
        functools.partial(_mem_sample_kernel, bs), grid=(n // bs,),
        in_specs=[tok, cache, cache], out_specs=tok,
        out_shape=jax.ShapeDtypeStruct((n, MEM_HEADS, MEM_HEAD_DIM), F32),
        compiler_params=_cparams(("parallel",), _mib(40)), name="mem_attend_sample",
    )(q, cache_mk, cache_mv)


def _pack_w_in(w_in):
    lead = w_in.shape[:-1]
    pts = np.cumsum([A_WIDTH, A_WIDTH, A_WIDTH, IDX_HEADS * IDX_DIM, IDX_DIM, IDX_HEADS, S5_WIDTH]).tolist()
    q, k, v, iq, ik, iw, u5, zc = jnp.split(w_in, pts, axis=-1)
    pad = jnp.zeros(lead + (IKW_PAD,), w_in.dtype)
    return jnp.concatenate([q, k, v, iq, ik, iw, pad, u5, zc], axis=-1).astype(BF16)


def _rope_tables(pos):
    half = A_HEAD_DIM // 2
    inv_freq = ROPE_THETA ** (-jnp.arange(half, dtype=F32) / half)
    ang = pos.astype(F32)[:, None] * inv_freq[None, :]
    c, s = jnp.cos(ang), jnp.sin(ang)
    return jnp.tile(c, (1, LANES // half)), jnp.concatenate([-s, s] * (LANES // A_HEAD_DIM), axis=1)


def _s5_matrices(lam_re, lam_im, log_dt, b_re, b_im, c_re, c_im, d):
    dt = jnp.exp(log_dt)[:, None]
    mag = jnp.exp(dt * lam_re)
    ang = dt * lam_im
    a_re = mag * jnp.cos(ang)
    a_im = mag * jnp.sin(ang)
    den = jnp.square(lam_re) + jnp.square(lam_im)
    z_re = ((a_re - 1.0) * lam_re + a_im * lam_im) / den
    z_im = (a_im * lam_re - (a_re - 1.0) * lam_im) / den
    bb_re = z_re[..., None] * b_re - z_im[..., None] * b_im
    bb_im = z_re[..., None] * b_im + z_im[..., None] * b_re
    eye = jnp.eye(S5_GROUPS, dtype=F32)
    in_re = jnp.einsum("gph,gk->ghkp", bb_re, eye).reshape(S5_WIDTH, S5_FLAT)
    in_im = jnp.einsum("gph,gk->ghkp", bb_im, eye).reshape(S5_WIDTH, S5_FLAT)
    bblk = jnp.concatenate([in_re, in_im], axis=1).astype(BF16)
    out_re = jnp.einsum("ghp,gk->gpkh", c_re, eye).reshape(S5_FLAT, S5_WIDTH)
    out_im = jnp.einsum("ghp,gk->gpkh", -c_im, eye).reshape(S5_FLAT, S5_WIDTH)
    cblk = jnp.concatenate([out_re, out_im], axis=0).astype(BF16)
    avec = jnp.concatenate([a_re.reshape(1, S5_FLAT), a_im.reshape(1, S5_FLAT)], axis=1)
    return bblk, cblk, d.reshape(1, S5_WIDTH), avec


def _pick_tile(n, pref):
    t = min(n, pref)
    assert n % t == 0
    return t


def kernel(x_prompt, x_sample, cache_k, cache_v, cache_kidx, cache_mem_k, cache_mem_v, state_s5_re, state_s5_im,
           page_table, mem_prompt, w_in, w_a_out, s5_lambda_re, s5_lambda_im, s5_log_dt, s5_b_re, s5_b_im,
           s5_c_re, s5_c_im, s5_d, w_glu_a, w_glu_b, sg_ln_g, sg_ln_b, sg_w, sg_b, w_c_out, w_gate, b_gate,
           w_mix_out, ln1_g, ln1_b, w_mq, w_mk, w_mv, w_mo, ln2_g, ln2_b, w_ff1, w_ff2, ln3_g, ln3_b):
    depth = w_in.shape[0]
    bp, tp, _ = x_prompt.shape
    db, ds, _ = x_sample.shape
    assert ds == 1 and tp % S5_CHUNK == 0 and tp % SG_CHUNK == 0 and tp % KEY_CHUNK == 0
    n_pages = page_table.shape[1]
    past = n_pages * PAGE_SIZE
    alpha = (2 * depth) ** 0.25
    bf = lambda a: a.astype(BF16)
    row = lambda a, l: a[l][None, :]

    w_in_p = _pack_w_in(w_in)
    w_a_out_b, w_glu_a_b, w_glu_b_b, w_c_out_b = bf(w_a_out), bf(w_glu_a), bf(w_glu_b), bf(w_c_out)
    w_gate_b, w_mix_b = bf(w_gate), bf(w_mix_out)
    w_mq_b, w_mk_b, w_mv_b, w_mo_b = bf(w_mq), bf(w_mk), bf(w_mv), bf(w_mo)
    w_ff1_b, w_ff2_b = bf(w_ff1), bf(w_ff2)
    s5m = [_s5_matrices(s5_lambda_re[l], s5_lambda_im[l], s5_log_dt[l], s5_b_re[l], s5_b_im[l], s5_c_re[l],
                        s5_c_im[l], s5_d[l]) for l in range(depth)]

    n = bp * tp
    tm = _pick_tile(tp, 256)
    cos_p, sin_p = _rope_tables(jnp.arange(tp))
    k_sel_p = min(TOPK_MAX, tp // 4)
    qb = _pick_tile(tp, 512)
    sg_bias = [jnp.repeat(sg_b[l].T, SG_GROUP_CH, axis=1) for l in range(depth)]
    mem2d = mem_prompt.reshape(bp * MEM_TOKENS, D_MODEL)
    h0 = jnp.zeros((bp, 1, 2 * S5_FLAT), F32)

    h = x_prompt.reshape(n, D_MODEL)
    ktp, vtp, iktp, mkp, mvp, s5p = [], [], [], [], [], []
    for l in range(depth):
        bblk, cblk, dvec, avec = s5m[l]
        q, iq, u5, zc, kbf, ikbf, kt, vt, vtbf, ikwt = _proj(h, w_in_p[l], cos_p, sin_p, tp, tm, False)
        r3 = lambda a: a.reshape(bp, tp, a.shape[-1])
        att = _dsa_prompt(r3(q), r3(iq), ikwt, r3(ikbf), r3(kbf), vtbf, k_sel_p, qb)
        u5p = u5.reshape(bp, tp // S5_CHUNK, SUBLANES, S5_SEG, S5_WIDTH).swapaxes(2, 3).reshape(bp, tp, S5_WIDTH)
        y5p, hfin = _s5_prompt(u5p, bblk, cblk, dvec, avec, h0)
        y5 = y5p.reshape(bp, tp // S5_CHUNK, S5_SEG, SUBLANES, S5_WIDTH).swapaxes(2, 3)
        sg = _sgate_prompt(zc, row(sg_ln_g, l), row(sg_ln_b, l), sg_w[l], sg_bias[l], _pick_tile(tp, 512))
        h = _merge(alpha, h, att.reshape(n, A_WIDTH), y5.reshape(n, S5_WIDTH), sg, w_a_out_b[l], w_glu_a_b[l],
                   w_glu_b_b[l], w_c_out_b[l], w_gate_b[l], row(b_gate, l), w_mix_b[l], row(ln1_g, l), row(ln1_b, l), tm)
        mk, mv = _two_linear(mem2d, w_mk_b[l], w_mv_b[l], _pick_tile(bp * MEM_TOKENS, 256))
        h = _mem_prompt(alpha, h.reshape(bp, tp, D_MODEL), mk.reshape(bp, MEM_TOKENS, MEM_WIDTH),
                        mv.reshape(bp, MEM_TOKENS, MEM_WIDTH), w_mq_b[l], w_mo_b[l], row(ln2_g, l), row(ln2_b, l),
                        tm).reshape(n, D_MODEL)
        h = _ffn(alpha, h, w_ff1_b[l], w_ff2_b[l], row(ln3_g, l), row(ln3_b, l), tm)
        ktp.append(kt)
        vtp.append(vt)
        iktp.append(ikwt[:, :IDX_DIM, :])
        mkp.append(mk.reshape(bp, MEM_TOKENS, MEM_HEADS, MEM_HEAD_DIM))
        mvp.append(mv.reshape(bp, MEM_TOKENS, MEM_HEADS, MEM_HEAD_DIM))
        s5p.append(hfin.reshape(bp, 2, S5_GROUPS, S5_STATE))
    y_prompt = h.reshape(bp, tp, D_MODEL)

    cache_k_t = jnp.transpose(cache_k, (0, 1, 3, 4, 2))
    cache_v_t = jnp.transpose(cache_v, (0, 1, 3, 4, 2))
    cache_kidx_t = jnp.swapaxes(cache_kidx, 2, 3)
    cos_s, sin_s = _rope_tables(jnp.full((db,), past, jnp.int32))
    k_sel_s = min(TOPK_MAX, (past + ds) // 4)
    h = x_sample.reshape(db, D_MODEL)
    kts, vts, ikts, s5s, sgs = [], [], [], [], []
    for l in range(depth):
        bblk, cblk, dvec, avec = s5m[l]
        q, iq, u5, zc, _, _, kt, vt, _, ikwt, k, v, ikw = _proj(h, w_in_p[l], cos_s, sin_s, db, db, True)
        head_pad = ((0, 0), (0, SUBLANES - IDX_HEADS), (0, 0))
        scores = _sample_scores(page_table, jnp.pad(iq.reshape(db, IDX_HEADS, IDX_DIM), head_pad),
                                jnp.pad(ikw[:, IDX_DIM:IDX_DIM + IDX_HEADS].reshape(db, IDX_HEADS, 1), head_pad),
                                ikw.reshape(db, 1, LANES), cache_kidx_t, l, _pick_tile(db, 4))
        mask = _sample_select(scores.reshape(db, past + LANES).T, k_sel_s).T
        hd = lambda a: a.reshape(db, A_HEADS, A_HEAD_DIM)
        att = _sample_attend(page_table, hd(q), hd(k), hd(v), mask.reshape(db, 1, past + LANES), cache_k_t, cache_v_t, l)
        h0s = jnp.concatenate([state_s5_re[l].reshape(db, S5_FLAT), state_s5_im[l].reshape(db, S5_FLAT)], axis=1)
        wd = jnp.repeat(sg_w[l][:, 0, 0], SG_GROUP_CH)[None, :]
        bd = jnp.repeat(sg_b[l][:, 0], SG_GROUP_CH)[None, :]
        y5, hnew, sg, vsg = _sample_s5_sg(u5, zc, h0s, bblk, cblk, dvec, avec, row(sg_ln_g, l), row(sg_ln_b, l), wd, bd)
        h = _merge(alpha, h, att.reshape(db, A_WIDTH), y5, sg, w_a_out_b[l], w_glu_a_b[l], w_glu_b_b[l], w_c_out_b[l],
                   w_gate_b[l], row(b_gate, l), w_mix_b[l], row(ln1_g, l), row(ln1_b, l), db)
        mq = _linear(h, w_mq_b[l])
        mo = _mem_sample(mq.reshape(db, MEM_HEADS, MEM_HEAD_DIM), cache_mem_k, cache_mem_v, l, 4)
        h = _linear_res_ln(alpha, mo.reshape(db, MEM_WIDTH), w_mo_b[l], h, row(ln2_g, l), row(ln2_b, l))
        h = _ffn(alpha, h, w_ff1_b[l], w_ff2_b[l], row(ln3_g, l), row(ln3_b, l), db)
        kts.append(kt)
        vts.append(vt)
        ikts.append(ikwt[:, :IDX_DIM, :])
        s5s.append(hnew.reshape(db, 2, S5_GROUPS, S5_STATE))
        sgs.append(vsg.reshape(db, ds, SG_WIDTH))
    y_sample = h.reshape(db, ds, D_MODEL)

    def heads_last(stack, batch, seq):
        return jnp.transpose(stack.reshape(depth, batch, A_HEADS, A_HEAD_DIM, seq), (0, 1, 4, 2, 3))

    s5p, s5s = jnp.stack(s5p), jnp.stack(s5s)
    return (y_prompt, y_sample,
            heads_last(jnp.stack(ktp), bp, tp), heads_last(jnp.stack(vtp), bp, tp), jnp.swapaxes(jnp.stack(iktp), 2, 3),
            jnp.stack(mkp), jnp.stack(mvp), s5p[:, :, 0], s5p[:, :, 1],
            heads_last(jnp.stack(kts), 1, db).reshape(depth, db, ds, A_HEADS, A_HEAD_DIM),
            heads_last(jnp.stack(vts), 1, db).reshape(depth, db, ds, A_HEADS, A_HEAD_DIM),
            jnp.swapaxes(jnp.stack(ikts), 2, 3).reshape(depth, db, ds, IDX_DIM),
            s5s[:, :, 0], s5s[:, :, 1], jnp.stack(sgs))
```

```python
import functools
import math

import numpy as np
import jax
import jax.numpy as jnp
from jax import lax
from jax.experimental import pallas as pl
from jax.experimental.pallas import tpu as pltpu

F32 = jnp.float32
BF16 = jnp.bfloat16
I32 = jnp.int32

D_MODEL = 1024
A_HEADS = 8
A_HEAD_DIM = 64
A_WIDTH = A_HEADS * A_HEAD_DIM
IDX_HEADS = 4
IDX_DIM = 64
TOPK_MAX = 256
ROPE_THETA = 10000.0
S5_GROUP_CH = 16
S5_GROUPS = 16
S5_WIDTH = S5_GROUPS * S5_GROUP_CH
S5_STATE = 64
S5_FLAT = S5_GROUPS * S5_STATE
SG_CHUNK = 128
SG_GROUPS = 4
SG_WIDTH = 256
SG_GROUP_CH = SG_WIDTH // SG_GROUPS
MEM_TOKENS = 256
MEM_HEADS = 4
MEM_HEAD_DIM = 128
MEM_WIDTH = MEM_HEADS * MEM_HEAD_DIM
D_FF = 4 * D_MODEL
N_BRANCH = 3
LN_EPS = 1e-5
PAGE_SIZE = 128

LANES = 128
SUBLANES = 8
VMEM_PHYSICAL_BYTES = 64 * 1024 * 1024

IKW_PAD = LANES - IDX_DIM - IDX_HEADS
OFF_Q = 0
OFF_K = OFF_Q + A_WIDTH
OFF_V = OFF_K + A_WIDTH
OFF_IQ = OFF_V + A_WIDTH
OFF_IKW = OFF_IQ + IDX_HEADS * IDX_DIM
OFF_U5 = OFF_IKW + LANES
OFF_ZC = OFF_U5 + S5_WIDTH
IN_PACKED = OFF_ZC + 2 * SG_WIDTH

INT_MIN = -2 ** 31
LOG2_E = math.log2(math.e)
NEG_INF_KEY = int(np.array([0xFF800000 ^ 0x7FFFFFFF], dtype=np.uint32).view(np.int32)[0])


def _cparams(semantics, vmem_bytes):
    assert vmem_bytes < VMEM_PHYSICAL_BYTES
    return pltpu.CompilerParams(dimension_semantics=semantics, vmem_limit_bytes=int(vmem_bytes))


def _mib(n):
    return n * 1024 * 1024


def _const_spec(shape):
    n = len(shape)
    return pl.BlockSpec(shape, lambda *_: (0,) * n, pipeline_mode=pl.Buffered(1))


def _layer_norm(x, g, b):
    mu = jnp.mean(x, axis=-1, keepdims=True)
    xc = x - mu
    var = jnp.mean(xc * xc, axis=-1, keepdims=True)
    return xc * lax.rsqrt(var + LN_EPS) * g + b


def _dot(a, b):
    return jnp.dot(a.astype(BF16), b.astype(BF16), preferred_element_type=F32)


def _dot_nt(a, b):
    return lax.dot_general(a.astype(BF16), b.astype(BF16), (((1,), (1,)), ((), ())),
                           preferred_element_type=F32)


def _proj_kernel(emit_rows, x_ref, w_ref, cos_ref, sin_ref, q_ref, iq_ref, u5_ref, zc_ref, kbf_ref, ikbf_ref,
                 kt_ref, vt_ref, vtbf_ref, ikwt_ref, *row_refs):
    y = _dot(x_ref[...], w_ref[...])
    cos = cos_ref[...]
    sin = sin_ref[...]
    lane = lax.broadcasted_iota(I32, cos.shape, 1)
    first_half = (lane & (A_HEAD_DIM // 2)) == 0

    def rope(t, c, s):
        partner = jnp.where(first_half, pltpu.roll(t, LANES - A_HEAD_DIM // 2, 1), pltpu.roll(t, A_HEAD_DIM // 2, 1))
        return t * c + partner * s

    for g in range(A_WIDTH // LANES):
        sl = slice(g * LANES, (g + 1) * LANES)
        q_ref[:, sl] = rope(y[:, OFF_Q + g * LANES:OFF_Q + (g + 1) * LANES], cos, sin)
        kg = rope(y[:, OFF_K + g * LANES:OFF_K + (g + 1) * LANES], cos, sin)
        kbf_ref[:, sl] = kg.astype(BF16)
        kt_ref[0, sl, :] = kg.T
        vg = y[:, OFF_V + g * LANES:OFF_V + (g + 1) * LANES]
        vt = vg.T
        vt_ref[0, sl, :] = vt
        vtbf_ref[0, sl, :] = vt.astype(BF16)
        if emit_rows:
            row_refs[0][:, sl] = kg
            row_refs[1][:, sl] = vg
    for g in range(IDX_HEADS * IDX_DIM // LANES):
        iq_ref[:, g * LANES:(g + 1) * LANES] = rope(y[:, OFF_IQ + g * LANES:OFF_IQ + (g + 1) * LANES], cos, sin)
    is_key = lane < IDX_DIM
    ikw = rope(y[:, OFF_IKW:OFF_IKW + LANES], jnp.where(is_key, cos, 1.0), jnp.where(is_key, sin, 0.0))
    ikbf_ref[...] = ikw[:, :IDX_DIM].astype(BF16)
    ikwt_ref[0] = ikw.T
    if emit_rows:
        row_refs[2][...] = ikw
    u5_ref[...] = y[:, OFF_U5:OFF_U5 + S5_WIDTH]
    zc_ref[...] = y[:, OFF_ZC:OFF_ZC + 2 * SG_WIDTH]


def _proj(x, w, cos, sin, seq, tm, emit_rows):
    n = x.shape[0]
    nt = seq // tm
    row = lambda w_: pl.BlockSpec((tm, w_), lambda i: (i, 0))
    tab = pl.BlockSpec((tm, LANES), lambda i: (i % nt, 0))
    tr = lambda w_: pl.BlockSpec((1, w_, tm), lambda i: (i // nt, 0, i % nt))
    sds = jax.ShapeDtypeStruct
    row_out = [(A_WIDTH, F32), (IDX_HEADS * IDX_DIM, F32), (S5_WIDTH, F32), (2 * SG_WIDTH, F32), (A_WIDTH, BF16),
               (IDX_DIM, BF16)]
    tr_out = [(A_WIDTH, F32), (A_WIDTH, F32), (A_WIDTH, BF16), (LANES, F32)]
    extra = [(A_WIDTH, F32), (A_WIDTH, F32), (LANES, F32)] if emit_rows else []
    out_shape = ([sds((n, w_), dt) for w_, dt in row_out] + [sds((n // seq, w_, seq), dt) for w_, dt in tr_out]
                 + [sds((n, w_), dt) for w_, dt in extra])
    out_specs = [row(w_) for w_, _ in row_out] + [tr(w_) for w_, _ in tr_out] + [row(w_) for w_, _ in extra]
    return pl.pallas_call(
        functools.partial(_proj_kernel, emit_rows), grid=(n // tm,),
        in_specs=[row(D_MODEL), _const_spec((D_MODEL, IN_PACKED)), tab, tab],
        out_specs=out_specs, out_shape=out_shape,
        compiler_params=_cparams(("parallel",), _mib(40)), name="in_proj_rope",
    )(x, w, cos, sin)


KEY_CHUNK = 512


def _score_keys(score):
    score = jnp.where(score == 0.0, 0.0, score)
    bits = lax.bitcast_convert_type(score, I32)
    return jnp.where(bits < 0, bits ^ 0x7FFFFFFF, bits)


def _rows_reduce(v, op):
    accs = [None] * 4
    for i in range(v.shape[0] // SUBLANES):
        part = v[i * SUBLANES:(i + 1) * SUBLANES]
        accs[i % 4] = part if accs[i % 4] is None else op(accs[i % 4], part)
    return op(op(accs[0], accs[1]), op(accs[2], accs[3]))


def _chunk_rows(c):
    return pl.ds(pl.multiple_of(c * KEY_CHUNK, KEY_CHUNK), KEY_CHUNK)


def _count(key_ref, n_chunks, pred, extra=None):
    r = key_ref.shape[1]
    tile = SUBLANES * (4 // key_ref.dtype.itemsize)

    def body(c, acc):
        ch = key_ref[_chunk_rows(c), :]
        accs = [acc, None, None, None]
        for i in range(KEY_CHUNK // tile):
            v = pred(ch[i * tile:(i + 1) * tile], c * KEY_CHUNK + i * tile)
            accs[i % 4] = v if accs[i % 4] is None else accs[i % 4] + v
        return (accs[0] + accs[1]) + (accs[2] + accs[3])

    acc = lax.fori_loop(0, n_chunks, body, jnp.zeros((tile, r), key_ref.dtype))
    cnt = jnp.sum(acc.astype(I32), axis=0, keepdims=True)
    return cnt if extra is None else cnt + extra


def _as_int(mask):
    return jnp.where(mask, 1, 0)


def _as_i16(mask):
    return jnp.where(mask, jnp.int16(1), jnp.int16(0))


HALF = 1 << 15


def _search16(half_ref, n_chunks, need, xhalf=None):
    r = half_ref.shape[1]

    def bit_body(i, prefix):
        cand = (prefix | jnp.left_shift(jnp.int32(1), 15 - i)) - HALF
        cand16 = cand.astype(jnp.int16)
        cnt = _count(half_ref, n_chunks, lambda ch, _: _as_i16(ch >= cand16),
                     None if xhalf is None else _as_int(xhalf >= cand))
        return jnp.where(cnt >= need, cand + HALF, prefix)

    return lax.fori_loop(0, 16, bit_body, jnp.zeros((1, r), I32)) - HALF


def _topk_threshold(key_ref, half_ref, n_chunks, k_sel, last_row, xkey=None):
    r = key_ref.shape[1]

    def split(k):
        return jnp.right_shift(k, 16), (k & 0xFFFF) - HALF

    def fill_hi(c, carry):
        half_ref[_chunk_rows(c), :] = split(key_ref[_chunk_rows(c), :])[0].astype(jnp.int16)
        return carry

    lax.fori_loop(0, n_chunks, fill_hi, 0)
    xhi, xlo = (None, None) if xkey is None else split(xkey)
    t_hi = _search16(half_ref, n_chunks, k_sel, xhi)
    t_hi16 = t_hi.astype(jnp.int16)
    above = _count(half_ref, n_chunks, lambda ch, _: _as_i16(ch > t_hi16), None if xkey is None else _as_int(xhi > t_hi))

    def fill_lo(c, carry):
        hi, lo = split(key_ref[_chunk_rows(c), :])
        half_ref[_chunk_rows(c), :] = jnp.where(hi == t_hi, lo, -HALF).astype(jnp.int16)
        return carry

    lax.fori_loop(0, n_chunks, fill_lo, 0)
    t_lo = _search16(half_ref, n_chunks, k_sel - above, None if xkey is None else jnp.where(xhi == t_hi, xlo, -HALF))
    thr = t_hi * (2 * HALF) + (t_lo + HALF)

    cnt_gt = _count(key_ref, n_chunks, lambda ch, _: _as_int(ch > thr), None if xkey is None else _as_int(xkey > thr))
    cnt_eq = _count(key_ref, n_chunks, lambda ch, _: _as_int(ch == thr), None if xkey is None else _as_int(xkey == thr))
    need = k_sel - cnt_gt
    surplus = jnp.where(cnt_eq > need, _as_int(thr > NEG_INF_KEY), 0)
    any_surplus = jnp.max(surplus) > 0
    row = lax.broadcasted_iota(I32, (SUBLANES, r), 0)
    n_bits = int(last_row).bit_length()

    def tie_search():
        def body(i, lo):
            cand = lo + jnp.left_shift(jnp.int32(1), n_bits - 1 - i)
            extra = None if xkey is None else jnp.where(xkey == thr, _as_int(cand >= last_row), 0)
            cnt = _count(key_ref, n_chunks, lambda ch, r0: jnp.where(ch == thr, _as_int(row + r0 <= cand), 0), extra)
            return jnp.where(cnt < need, cand, lo)

        return lax.fori_loop(0, n_bits, body, jnp.full((1, r), -1, I32)) + 1

    jstar = lax.cond(any_surplus, tie_search, lambda: jnp.full((1, r), last_row, I32))
    return thr, jstar


def _selected(ch, row, thr, jstar):
    return jnp.where(ch == thr, _as_int(row <= jstar), _as_int(ch > thr)) > 0


def _dsa_prompt_kernel(k_sel, q_ref, iq_ref, iwt_ref, ikbf_ref, kbf_ref, vtbf_ref, o_ref, key_scr, half_scr, bias_scr,
                       qpair_scr, m_scr, l_scr, acc_scr):
    qb = q_ref.shape[1]
    t = kbf_ref.shape[1]
    q0 = pl.program_id(1) * qb
    n_chunks = (q0 + qb + KEY_CHUNK - 1) // KEY_CHUNK
    qcol = q0 + lax.broadcasted_iota(I32, (1, qb), 1)
    row = lax.broadcasted_iota(I32, (KEY_CHUNK, qb), 0)
    iwt = iwt_ref[0][:IDX_HEADS] * (IDX_HEADS ** -0.5)
    iq = iq_ref[0] * (IDX_DIM ** -0.5)
    iqh = [iq[:, h * IDX_DIM:(h + 1) * IDX_DIM].astype(BF16) for h in range(IDX_HEADS)]

    def score_chunk(c, carry):
        ik = ikbf_ref[0, _chunk_rows(c), :]
        score = jnp.zeros((KEY_CHUNK, qb), F32)
        for h in range(IDX_HEADS):
            score = score + iwt[h:h + 1, :] * jnp.maximum(_dot_nt(ik, iqh[h]), 0.0)
        key_scr[_chunk_rows(c), :] = _score_keys(jnp.where(row + c * KEY_CHUNK <= qcol, score, -jnp.inf))
        return carry

    lax.fori_loop(0, n_chunks, score_chunk, 0)
    thr, jstar = _topk_threshold(key_scr, half_scr, n_chunks, k_sel, t - 1)

    def bias_chunk(c, carry):
        krow = row + c * KEY_CHUNK
        sel = _selected(key_scr[_chunk_rows(c), :], krow, thr, jstar)
        bias_scr[_chunk_rows(c), :] = jnp.where(krow <= qcol, jnp.where(sel, 0.0, -jnp.inf), -jnp.inf)
        return carry

    lax.fori_loop(0, n_chunks, bias_chunk, 0)

    qs = q_ref[0] * (A_HEAD_DIM ** -0.5 * LOG2_E)
    lane = lax.broadcasted_iota(I32, (qb, LANES), 1)
    for g in range(A_WIDTH // LANES):
        qg = qs[:, g * LANES:(g + 1) * LANES]
        qpair_scr[g, :qb, :] = jnp.where(lane < A_HEAD_DIM, qg, 0.0).astype(BF16)
        qpair_scr[g, qb:, :] = jnp.where(lane < A_HEAD_DIM, 0.0, qg).astype(BF16)
    m_scr[...] = jnp.full(m_scr.shape, -jnp.inf, F32)
    l_scr[...] = jnp.zeros(l_scr.shape, F32)
    acc_scr[...] = jnp.zeros(acc_scr.shape, F32)

    def attend_chunk(c, carry):
        rows = _chunk_rows(c)
        bias = bias_scr[rows, :]
        for g in range(A_WIDTH // LANES):
            s_pair = _dot_nt(kbf_ref[0, rows, g * LANES:(g + 1) * LANES], qpair_scr[g])
            for half in range(2):
                h = 2 * g + half
                s = s_pair[:, half * qb:(half + 1) * qb] + bias
                m_old = m_scr[h:h + 1, :]
                m_new = jnp.maximum(m_old, jnp.max(_rows_reduce(s, jnp.maximum), axis=0, keepdims=True))
                m_use = jnp.where(m_new == -jnp.inf, 0.0, m_new)
                p = jnp.exp2(s - m_use)
                resc = jnp.exp2(m_old - m_use)
                m_scr[h:h + 1, :] = m_new
                l_scr[h:h + 1, :] = resc * l_scr[h:h + 1, :] + jnp.sum(_rows_reduce(p, jnp.add), axis=0, keepdims=True)
                hd = slice(h * A_HEAD_DIM, (h + 1) * A_HEAD_DIM)
                acc_scr[hd, :] = resc * acc_scr[hd, :] + _dot(vtbf_ref[0, hd, rows], p)
        return carry

    lax.fori_loop(0, n_chunks, attend_chunk, 0)
    out_row = lax.broadcasted_iota(I32, (LANES, qb), 0)
    for g in range(A_WIDTH // LANES):
        l_pair = jnp.where(out_row < A_HEAD_DIM, l_scr[2 * g:2 * g + 1, :], l_scr[2 * g + 1:2 * g + 2, :])
        o_ref[0, :, g * LANES:(g + 1) * LANES] = (acc_scr[g * LANES:(g + 1) * LANES, :] / l_pair).T


def _dsa_prompt(q, iq, ikwt, ikbf, kbf, vtbf, k_sel, qb):
    b, t, _ = q.shape
    blk = lambda w_: pl.BlockSpec((1, qb, w_), lambda i, j: (i, j, 0))
    full = lambda w_: pl.BlockSpec((1, t, w_), lambda i, j: (i, 0, 0))
    return pl.pallas_call(
        functools.partial(_dsa_prompt_kernel, k_sel), grid=(b, t // qb),
        in_specs=[blk(A_WIDTH), blk(IDX_HEADS * IDX_DIM),
                  pl.BlockSpec((1, SUBLANES, qb), lambda i, j: (i, IDX_DIM // SUBLANES, j)),
                  full(IDX_DIM), full(A_WIDTH), pl.BlockSpec((1, A_WIDTH, t), lambda i, j: (i, 0, 0))],
        out_specs=blk(A_WIDTH), out_shape=jax.ShapeDtypeStruct((b, t, A_WIDTH), F32),
        scratch_shapes=[pltpu.VMEM((t, qb), I32), pltpu.VMEM((t, qb), jnp.int16), pltpu.VMEM((t, qb), F32),
                        pltpu.VMEM((A_WIDTH // LANES, 2 * qb, LANES), BF16), pltpu.VMEM((A_HEADS, qb), F32),
                        pltpu.VMEM((A_HEADS, qb), F32), pltpu.VMEM((A_WIDTH, qb), F32)],
        compiler_params=_cparams(("parallel", "arbitrary"), _mib(48)), name="dsa_prompt",
    )(q, iq, ikwt, ikbf, kbf, vtbf)


S5_SEG = 64
S5_CHUNK = S5_SEG * SUBLANES


def _s5_prompt_kernel(u_ref, bblk_ref, cblk_ref, d_ref, a_ref, h0_ref, y_ref, hout_ref,
                      h_scr, pw_scr, hin_scr, carry_scr):
    c = pl.program_id(1)
    a = a_ref[...]
    a_re1, a_im1 = a[:, :S5_FLAT], a[:, S5_FLAT:]

    @pl.when(c == 0)
    def _():
        carry_scr[...] = h0_ref[0]
        pr, pi = a_re1, a_im1
        for k in range(S5_SEG):
            pw_scr[k:k + 1, :S5_FLAT] = pr
            pw_scr[k:k + 1, S5_FLAT:] = pi
            pr, pi = pr * a_re1 - pi * a_im1, pr * a_im1 + pi * a_re1

    up = u_ref[0]
    h_scr[...] = _dot(up, bblk_ref[...])

    a_re = jnp.broadcast_to(a_re1, (SUBLANES, S5_FLAT))
    a_im = jnp.broadcast_to(a_im1, (SUBLANES, S5_FLAT))

    def scan_step(k, carry):
        hr, hi = carry
        rows = pl.ds(pl.multiple_of(k * SUBLANES, SUBLANES), SUBLANES)
        nhr = a_re * hr - a_im * hi + h_scr[rows, :S5_FLAT]
        nhi = a_re * hi + a_im * hr + h_scr[rows, S5_FLAT:]
        h_scr[rows, :S5_FLAT] = nhr
        h_scr[rows, S5_FLAT:] = nhi
        return nhr, nhi

    zero = jnp.zeros((SUBLANES, S5_FLAT), F32)
    end_re, end_im = lax.fori_loop(0, S5_SEG, scan_step, (zero, zero))

    p_last_re, p_last_im = pw_scr[S5_SEG - 1:S5_SEG, :S5_FLAT], pw_scr[S5_SEG - 1:S5_SEG, S5_FLAT:]
    cr, ci = carry_scr[:, :S5_FLAT], carry_scr[:, S5_FLAT:]
    for j in range(SUBLANES):
        hin_scr[j:j + 1, :S5_FLAT] = cr
        hin_scr[j:j + 1, S5_FLAT:] = ci
        cr, ci = (end_re[j:j + 1] + p_last_re * cr - p_last_im * ci,
                  end_im[j:j + 1] + p_last_re * ci + p_last_im * cr)
    carry_scr[:, :S5_FLAT] = cr
    carry_scr[:, S5_FLAT:] = ci
    hin_re, hin_im = hin_scr[:, :S5_FLAT], hin_scr[:, S5_FLAT:]

    def fix_step(k, _):
        rows = pl.ds(pl.multiple_of(k * SUBLANES, SUBLANES), SUBLANES)
        pr = jnp.broadcast_to(pw_scr[pl.ds(k, 1), :S5_FLAT], (SUBLANES, S5_FLAT))
        pi = jnp.broadcast_to(pw_scr[pl.ds(k, 1), S5_FLAT:], (SUBLANES, S5_FLAT))
        h_scr[rows, :S5_FLAT] = h_scr[rows, :S5_FLAT] + pr * hin_re - pi * hin_im
        h_scr[rows, S5_FLAT:] = h_scr[rows, S5_FLAT:] + pr * hin_im + pi * hin_re
        return 0

    lax.fori_loop(0, S5_SEG, fix_step, 0)
    y_ref[0] = _dot(h_scr[...], cblk_ref[...]) + d_ref[...] * up

    @pl.when(c == pl.num_programs(1) - 1)
    def _():
        hout_ref[0] = carry_scr[...]


def _s5_prompt(u, bblk, cblk, dvec, avec, h0):
    b, t, _ = u.shape
    blk = pl.BlockSpec((1, S5_CHUNK, S5_WIDTH), lambda i, j: (i, j, 0))
    st = pl.BlockSpec((1, 1, 2 * S5_FLAT), lambda i, j: (i, 0, 0))
    return pl.pallas_call(
        _s5_prompt_kernel, grid=(b, t // S5_CHUNK),
        in_specs=[blk, _const_spec((S5_WIDTH, 2 * S5_FLAT)), _const_spec((2 * S5_FLAT, S5_WIDTH)),
                  _const_spec((1, S5_WIDTH)), _const_spec((1, 2 * S5_FLAT)), st],
        out_specs=[blk, st],
        out_shape=[jax.ShapeDtypeStruct((b, t, S5_WIDTH), F32), jax.ShapeDtypeStruct((b, 1, 2 * S5_FLAT), F32)],
        scratch_shapes=[pltpu.VMEM((S5_CHUNK, 2 * S5_FLAT), F32),
                        pltpu.VMEM((S5_SEG, 2 * S5_FLAT), F32), pltpu.VMEM((SUBLANES, 2 * S5_FLAT), F32),
                        pltpu.VMEM((1, 2 * S5_FLAT), F32)],
        compiler_params=_cparams(("parallel", "arbitrary"), _mib(32)), name="s5_prompt",
    )(u, bblk, cblk, dvec, avec, h0)


def _sgate_prompt_kernel(zc_ref, g_ref, b_ref, w_ref, bias_ref, o_ref):
    z = jax.nn.gelu(zc_ref[...])
    u = z[:, :SG_WIDTH]
    v = _layer_norm(z[:, SG_WIDTH:], g_ref[...], b_ref[...])
    ri = lax.broadcasted_iota(I32, (SG_CHUNK, SG_CHUNK), 0)
    ci = lax.broadcasted_iota(I32, (SG_CHUNK, SG_CHUNK), 1)
    wm = [jnp.where(ci <= ri, w_ref[g], 0.0).astype(BF16) for g in range(SG_GROUPS)]
    low = lax.broadcasted_iota(I32, (SG_CHUNK, LANES), 1) < SG_GROUP_CH
    bias = bias_ref[...]
    for c in range(zc_ref.shape[0] // SG_CHUNK):
        rows = slice(c * SG_CHUNK, (c + 1) * SG_CHUNK)
        for pr in range(SG_WIDTH // LANES):
            cols = slice(pr * LANES, (pr + 1) * LANES)
            vp = v[rows, cols].astype(BF16)
            s = jnp.where(low, _dot(wm[2 * pr], vp), _dot(wm[2 * pr + 1], vp)) + bias[:, cols]
            o_ref[rows, cols] = u[rows, cols] * s


def _sgate_prompt(zc, g, b, w, bias, tm):
    n = zc.shape[0]
    return pl.pallas_call(
        _sgate_prompt_kernel, grid=(n // tm,),
        in_specs=[pl.BlockSpec((tm, 2 * SG_WIDTH), lambda i: (i, 0)), _const_spec((1, SG_WIDTH)),
                  _const_spec((1, SG_WIDTH)), _const_spec((SG_GROUPS, SG_CHUNK, SG_CHUNK)),
                  _const_spec((SG_CHUNK, SG_WIDTH))],
        out_specs=pl.BlockSpec((tm, SG_WIDTH), lambda i: (i, 0)),
        out_shape=jax.ShapeDtypeStruct((n, SG_WIDTH), F32),
        compiler_params=_cparams(("parallel",), _mib(24)), name="sgate_prompt",
    )(zc, g, b, w, bias)


def _merge_kernel(alpha, x_ref, att_ref, y5_ref, sg_ref, wa_ref, wga_ref, wgb_ref, wc_ref, wg_ref, bg_ref, wmix_ref,
                  g_ref, b_ref, o_ref):
    x = x_ref[...]
    gates = jax.nn.sigmoid(_dot(x, wg_ref[...]) + bg_ref[...])
    br_a = _dot(att_ref[...], wa_ref[...])
    y5 = jax.nn.gelu(y5_ref[...])
    br_b = _dot(y5, wga_ref[...]) * jax.nn.sigmoid(_dot(y5, wgb_ref[...]))
    br_c = _dot(sg_ref[...], wc_ref[...])
    merged = (gates[:, :D_MODEL] * br_a + gates[:, D_MODEL:2 * D_MODEL] * br_b + gates[:, 2 * D_MODEL:] * br_c)
    o_ref[...] = _layer_norm(alpha * x + _dot(merged, wmix_ref[...]), g_ref[...], b_ref[...])


def _merge(alpha, x, att, y5, sg, wa, wga, wgb, wc, wg, bg, wmix, g, b, tm):
    n = x.shape[0]
    row = lambda w_: pl.BlockSpec((tm, w_), lambda i: (i, 0))
    return pl.pallas_call(
        functools.partial(_merge_kernel, alpha), grid=(n // tm,),
        in_specs=[row(D_MODEL), row(A_WIDTH), row(S5_WIDTH), row(SG_WIDTH),
                  _const_spec((A_WIDTH, D_MODEL)), _const_spec((S5_WIDTH, D_MODEL)), _const_spec((S5_WIDTH, D_MODEL)),
                  _const_spec((SG_WIDTH, D_MODEL)), _const_spec((D_MODEL, N_BRANCH * D_MODEL)),
                  _const_spec((1, N_BRANCH * D_MODEL)), _const_spec((D_MODEL, D_MODEL)),
                  _const_spec((1, D_MODEL)), _const_spec((1, D_MODEL))],
        out_specs=row(D_MODEL), out_shape=jax.ShapeDtypeStruct((n, D_MODEL), F32),
        compiler_params=_cparams(("parallel",), _mib(48)), name="branch_merge",
    )(x, att, y5, sg, wa, wga, wgb, wc, wg, bg, wmix, g, b)


def _two_linear_kernel(x_ref, w1_ref, w2_ref, o1_ref, o2_ref):
    x = x_ref[...]
    o1_ref[...] = _dot(x, w1_ref[...])
    o2_ref[...] = _dot(x, w2_ref[...])


def _two_linear(x, w1, w2, tm):
    n, k = x.shape
    m = w1.shape[1]
    return pl.pallas_call(
        _two_linear_kernel, grid=(n // tm,),
        in_specs=[pl.BlockSpec((tm, k), lambda i: (i, 0)), _const_spec((k, m)), _const_spec((k, m))],
        out_specs=[pl.BlockSpec((tm, m), lambda i: (i, 0))] * 2,
        out_shape=[jax.ShapeDtypeStruct((n, m), F32)] * 2,
        compiler_params=_cparams(("parallel",), _mib(24)), name="mem_kv_proj",
    )(x, w1, w2)


def _mem_prompt_kernel(alpha, x_ref, mk_ref, mv_ref, wq_ref, wo_ref, g_ref, b_ref, o_ref):
    x = x_ref[0]
    q = _dot(x, wq_ref[...])
    mk = mk_ref[0].astype(BF16)
    mv = mv_ref[0].astype(BF16)
    outs = []
    for h in range(MEM_HEADS):
        sl = slice(h * MEM_HEAD_DIM, (h + 1) * MEM_HEAD_DIM)
        s = _dot_nt(q[:, sl], mk[:, sl]) * (MEM_HEAD_DIM ** -0.5)
        m = jnp.max(s, axis=1, keepdims=True)
        p = jnp.exp(s - m)
        l = jnp.sum(p, axis=1, keepdims=True)
        outs.append(_dot(p, mv[:, sl]) / l)
    o = jnp.concatenate(outs, axis=1)
    o_ref[0] = _layer_norm(alpha * x + _dot(o, wo_ref[...]), g_ref[...], b_ref[...])


def _mem_prompt(alpha, x, mk, mv, wq, wo, g, b, tm):
    bsz, t, _ = x.shape
    return pl.pallas_call(
        functools.partial(_mem_prompt_kernel, alpha), grid=(bsz, t // tm),
        in_specs=[pl.BlockSpec((1, tm, D_MODEL), lambda i, j: (i, j, 0)),
                  pl.BlockSpec((1, MEM_TOKENS, MEM_WIDTH), lambda i, j: (i, 0, 0)),
                  pl.BlockSpec((1, MEM_TOKENS, MEM_WIDTH), lambda i, j: (i, 0, 0)),
                  _const_spec((D_MODEL, MEM_WIDTH)), _const_spec((MEM_WIDTH, D_MODEL)),
                  _const_spec((1, D_MODEL)), _const_spec((1, D_MODEL))],
        out_specs=pl.BlockSpec((1, tm, D_MODEL), lambda i, j: (i, j, 0)),
        out_shape=jax.ShapeDtypeStruct((bsz, t, D_MODEL), F32),
        compiler_params=_cparams(("parallel", "parallel"), _mib(32)), name="mem_attend_prompt",
    )(x, mk, mv, wq, wo, g, b)


def _ffn_kernel(alpha, x_ref, w1_ref, w2_ref, g_ref, b_ref, o_ref):
    x = x_ref[...]
    hid = jnp.square(jnp.maximum(_dot(x, w1_ref[...]), 0.0))
    o_ref[...] = _layer_norm(alpha * x + _dot(hid, w2_ref[...]), g_ref[...], b_ref[...])


def _ffn(alpha, x, w1, w2, g, b, tm):
    n = x.shape[0]
    row = pl.BlockSpec((tm, D_MODEL), lambda i: (i, 0))
    return pl.pallas_call(
        functools.partial(_ffn_kernel, alpha), grid=(n // tm,),
        in_specs=[row, _const_spec((D_MODEL, D_FF)), _const_spec((D_FF, D_MODEL)),
                  _const_spec((1, D_MODEL)), _const_spec((1, D_MODEL))],
        out_specs=row, out_shape=jax.ShapeDtypeStruct((n, D_MODEL), F32),
        compiler_params=_cparams(("parallel",), _mib(56)), name="relu2_mlp",
    )(x, w1, w2, g, b)


def _sample_s5_sg_kernel(u_ref, zc_ref, h0_ref, bblk_ref, cblk_ref, d_ref, a_ref, g_ref, b_ref, wd_ref, bd_ref,
                         y_ref, h_ref, sg_ref, vsg_ref):
    u = u_ref[...]
    a = a_ref[...]
    a_re, a_im = a[:, :S5_FLAT], a[:, S5_FLAT:]
    h0 = h0_ref[...]
    h0_re, h0_im = h0[:, :S5_FLAT], h0[:, S5_FLAT:]
    bu = _dot(u, bblk_ref[...])
    h_re = bu[:, :S5_FLAT] + (a_re * h0_re - a_im * h0_im)
    h_im = bu[:, S5_FLAT:] + (a_re * h0_im + a_im * h0_re)
    h_ref[:, :S5_FLAT] = h_re
    h_ref[:, S5_FLAT:] = h_im
    y_ref[...] = _dot(h_ref[...], cblk_ref[...]) + d_ref[...] * u
    z = jax.nn.gelu(zc_ref[...])
    v = _layer_norm(z[:, SG_WIDTH:], g_ref[...], b_ref[...])
    vsg_ref[...] = v
    sg_ref[...] = z[:, :SG_WIDTH] * (wd_ref[...] * v + bd_ref[...])


def _sample_s5_sg(u, zc, h0, bblk, cblk, dvec, avec, g, b, wd, bd):
    n = u.shape[0]
    full = lambda a: _const_spec(a.shape)
    args = (u, zc, h0, bblk, cblk, dvec, avec, g, b, wd, bd)
    return pl.pallas_call(
        _sample_s5_sg_kernel, grid=(1,), in_specs=[full(a) for a in args],
        out_specs=[_const_spec((n, S5_WIDTH)), _const_spec((n, 2 * S5_FLAT)), _const_spec((n, SG_WIDTH)),
                   _const_spec((n, SG_WIDTH))],
        out_shape=[jax.ShapeDtypeStruct((n, S5_WIDTH), F32), jax.ShapeDtypeStruct((n, 2 * S5_FLAT), F32),
                   jax.ShapeDtypeStruct((n, SG_WIDTH), F32), jax.ShapeDtypeStruct((n, SG_WIDTH), F32)],
        compiler_params=_cparams(("arbitrary",), _mib(24)), name="sample_s5_sgate",
    )(*args)


def _sample_scores_kernel(n_pages, group, pt_ref, iq_ref, iw_ref, ikw_ref, *refs):
    page_refs, o_ref = refs[:group * n_pages], refs[group * n_pages]
    for g in range(group):
        iq = iq_ref[g]
        iw = iw_ref[g] * (IDX_HEADS ** -0.5)
        for j in range(n_pages):
            dots = _dot(iq, page_refs[g * n_pages + j][0, 0]) * (IDX_DIM ** -0.5)
            o_ref[g, :, j * PAGE_SIZE:(j + 1) * PAGE_SIZE] = jnp.sum(iw * jnp.maximum(dots, 0.0), axis=0, keepdims=True)
        own = jnp.sum(iq * ikw_ref[g][:, :IDX_DIM], axis=1, keepdims=True) * (IDX_DIM ** -0.5)
        own = jnp.sum(iw * jnp.maximum(own, 0.0), axis=0, keepdims=True)
        o_ref[g, :, n_pages * PAGE_SIZE:] = jnp.broadcast_to(own, (1, LANES))


def _sample_scores(page_table, iq, iw, ikw, cache_kidx_t, layer, group):
    n, n_pages = page_table.shape
    width = n_pages * PAGE_SIZE + LANES
    page = lambda g, j: pl.BlockSpec((1, 1, IDX_DIM, PAGE_SIZE), lambda i, pt: (layer, pt[i * group + g, j], 0, 0))
    grid_spec = pltpu.PrefetchScalarGridSpec(
        num_scalar_prefetch=1, grid=(n // group,),
        in_specs=[pl.BlockSpec((group, SUBLANES, IDX_DIM), lambda i, pt: (i, 0, 0)),
                  pl.BlockSpec((group, SUBLANES, 1), lambda i, pt: (i, 0, 0)),
                  pl.BlockSpec((group, 1, LANES), lambda i, pt: (i, 0, 0))]
        + [page(g, j) for g in range(group) for j in range(n_pages)],
        out_specs=pl.BlockSpec((group, 1, width), lambda i, pt: (i, 0, 0)))
    return pl.pallas_call(
        functools.partial(_sample_scores_kernel, n_pages, group), grid_spec=grid_spec,
        out_shape=jax.ShapeDtypeStruct((n, 1, width), F32),
        compiler_params=_cparams(("arbitrary",), _mib(24)), name="sample_idx_scores",
    )(page_table, iq, iw, ikw, *([cache_kidx_t] * (group * n_pages)))


def _sample_select_kernel(k_sel, sct_ref, o_ref, key_scr, half_scr):
    past, r = key_scr.shape
    n_chunks = past // KEY_CHUNK
    key_scr[...] = _score_keys(sct_ref[:past, :])
    xkey = _score_keys(sct_ref[past:past + 1, :])
    thr, jstar = _topk_threshold(key_scr, half_scr, n_chunks, k_sel, past, xkey)
    row = lax.broadcasted_iota(I32, (KEY_CHUNK, r), 0)
    for c in range(n_chunks):
        rows = slice(c * KEY_CHUNK, (c + 1) * KEY_CHUNK)
        o_ref[rows, :] = jnp.where(_selected(key_scr[rows, :], row + c * KEY_CHUNK, thr, jstar), 1.0, 0.0)
    own = jnp.where(_selected(xkey, jnp.full((1, r), past, I32), thr, jstar), 1.0, 0.0)
    o_ref[past:, :] = jnp.broadcast_to(own, (o_ref.shape[0] - past, r))


def _sample_select(scores_t, k_sel):
    rows, n = scores_t.shape
    past = rows - LANES
    assert past % KEY_CHUNK == 0
    return pl.pallas_call(
        functools.partial(_sample_select_kernel, k_sel), grid=(1,),
        in_specs=[_const_spec(scores_t.shape)], out_specs=_const_spec(scores_t.shape),
        out_shape=jax.ShapeDtypeStruct(scores_t.shape, F32),
        scratch_shapes=[pltpu.VMEM((past, n), I32), pltpu.VMEM((past, n), jnp.int16)],
        compiler_params=_cparams(("arbitrary",), _mib(24)), name="sample_topk_mask",
    )(scores_t)


def _sample_attend_kernel(n_pages, pt_ref, q_ref, kn_ref, vn_ref, m_ref, *refs):
    k_refs, v_refs, o_ref = refs[:n_pages], refs[n_pages:2 * n_pages], refs[2 * n_pages]
    q8 = q_ref[0] * (A_HEAD_DIM ** -0.5)
    q8b = q8.astype(BF16)
    head_of_row = lax.broadcasted_iota(I32, (A_HEADS, PAGE_SIZE), 0)
    pages = []
    for j in range(n_pages):
        sj = jnp.zeros((A_HEADS, PAGE_SIZE), F32)
        for h in range(A_HEADS):
            sj = jnp.where(head_of_row == h, _dot(q8b, k_refs[j][0, 0, h]), sj)
        pages.append(jnp.where(m_ref[0, :, j * PAGE_SIZE:(j + 1) * PAGE_SIZE] > 0.5, sj, -jnp.inf))
    s = jnp.concatenate(pages, axis=1)
    s_own = jnp.sum(q8 * kn_ref[0], axis=1, keepdims=True)
    s_own = jnp.where(m_ref[0, :, n_pages * PAGE_SIZE:n_pages * PAGE_SIZE + 1] > 0.5, s_own, -jnp.inf)
    m = jnp.maximum(jnp.max(s, axis=1, keepdims=True), s_own)
    p = jnp.exp(s - m)
    p_own = jnp.exp(s_own - m)
    l = jnp.sum(p, axis=1, keepdims=True) + p_own
    acc = p_own * vn_ref[0]
    head_of_out = lax.broadcasted_iota(I32, (A_HEADS, A_HEAD_DIM), 0)
    for j in range(n_pages):
        pj = p[:, j * PAGE_SIZE:(j + 1) * PAGE_SIZE].astype(BF16)
        for h in range(A_HEADS):
            acc = acc + jnp.where(head_of_out == h, _dot_nt(pj, v_refs[j][0, 0, h]), 0.0)
    o_ref[0] = acc / l


def _sample_attend(page_table, q, k_new, v_new, mask, cache_k_t, cache_v_t, layer):
    n, n_pages = page_table.shape
    page = lambda j: pl.BlockSpec((1, 1, A_HEADS, A_HEAD_DIM, PAGE_SIZE), lambda i, pt: (layer, pt[i, j], 0, 0, 0))
    tok = pl.BlockSpec((1, A_HEADS, A_HEAD_DIM), lambda i, pt: (i, 0, 0))
    grid_spec = pltpu.PrefetchScalarGridSpec(
        num_scalar_prefetch=1, grid=(n,),
        in_specs=[tok, tok, tok, pl.BlockSpec((1, 1, mask.shape[-1]), lambda i, pt: (i, 0, 0))]
        + [page(j) for j in range(n_pages)] * 2,
        out_specs=tok)
    return pl.pallas_call(
        functools.partial(_sample_attend_kernel, n_pages), grid_spec=grid_spec,
        out_shape=jax.ShapeDtypeStruct((n, A_HEADS, A_HEAD_DIM), F32),
        compiler_params=_cparams(("arbitrary",), _mib(40)), name="sample_dsa_attend",
    )(page_table, q, k_new, v_new, mask, *([cache_k_t] * n_pages), *([cache_v_t] * n_pages))


def _linear_kernel(x_ref, w_ref, o_ref):
    o_ref[...] = _dot(x_ref[...], w_ref[...])


def _linear(x, w):
    n, m = x.shape[0], w.shape[1]
    return pl.pallas_call(
        _linear_kernel, grid=(1,), in_specs=[_const_spec(x.shape), _const_spec(w.shape)],
        out_specs=_const_spec((n, m)), out_shape=jax.ShapeDtypeStruct((n, m), F32),
        compiler_params=_cparams(("arbitrary",), _mib(24)), name="linear",
    )(x, w)


def _linear_res_ln_kernel(alpha, o_ref, w_ref, x_ref, g_ref, b_ref, y_ref):
    y_ref[...] = _layer_norm(alpha * x_ref[...] + _dot(o_ref[...], w_ref[...]), g_ref[...], b_ref[...])


def _linear_res_ln(alpha, o, w, x, g, b):
    args = (o, w, x, g, b)
    return pl.pallas_call(
        functools.partial(_linear_res_ln_kernel, alpha), grid=(1,), in_specs=[_const_spec(a.shape) for a in args],
        out_specs=_const_spec(x.shape), out_shape=jax.ShapeDtypeStruct(x.shape, F32),
        compiler_params=_cparams(("arbitrary",), _mib(24)), name="linear_residual_norm",
    )(*args)


def _mem_sample_kernel(bs, q_ref, mk_ref, mv_ref, o_ref):
    for i in range(bs):
        q = q_ref[i]
        s = jnp.sum(mk_ref[0, i] * q[None], axis=-1, keepdims=True) * (MEM_HEAD_DIM ** -0.5)
        m = jnp.max(s, axis=0, keepdims=True)
        p = jnp.exp(s - m)
        l = jnp.sum(p, axis=0)
        o_ref[i] = jnp.sum(p * mv_ref[0, i], axis=0) / l


def _mem_sample(q, cache_mk, cache_mv, layer, bs):
    n = q.shape[0]
    cache = pl.BlockSpec((1, bs, MEM_TOKENS, MEM_HEADS, MEM_HEAD_DIM), lambda i: (layer, i, 0, 0, 0))
    tok = pl.BlockSpec((bs, MEM_HEADS, MEM_HEAD_DIM), lambda i: (i, 0, 0))
    return pl.pallas_call(
        functools.partial(_mem_sample_kernel, bs), grid=(n // bs,),
        in_specs=[tok, cache, cache], out_specs=tok,
        out_shape=jax.ShapeDtypeStruct((n, MEM_HEADS, MEM_HEAD_DIM), F32),
        compiler_params=_cparams(("parallel",), _mib(40)), name="mem_attend_sample",
    )(q, cache_mk, cache_mv)


def _pack_w_in(w_in):
    lead = w_in.shape[:-1]
    pts = np.cumsum([A_WIDTH, A_WIDTH, A_WIDTH, IDX_HEADS * IDX_DIM, IDX_DIM, IDX_HEADS, S5_WIDTH]).tolist()
    q, k, v, iq, ik, iw, u5, zc = jnp.split(w_in, pts, axis=-1)
    pad = jnp.zeros(lead + (IKW_PAD,), w_in.dtype)
    return jnp.concatenate([q, k, v, iq, ik, iw, pad, u5, zc], axis=-1).astype(BF16)


def _rope_tables(pos):
    half = A_HEAD_DIM // 2
    inv_freq = ROPE_THETA ** (-jnp.arange(half, dtype=F32) / half)
    ang = pos.astype(F32)[:, None] * inv_freq[None, :]
    c, s = jnp.cos(ang), jnp.sin(ang)
    return jnp.tile(c, (1, LANES // half)), jnp.concatenate([-s, s] * (LANES // A_HEAD_DIM), axis=1)


def _s5_matrices(lam_re, lam_im, log_dt, b_re, b_im, c_re, c_im, d):
    dt = jnp.exp(log_dt)[:, None]
    mag = jnp.exp(dt * lam_re)
    ang = dt * lam_im
    a_re = mag * jnp.cos(ang)
    a_im = mag * jnp.sin(ang)
    den = jnp.square(lam_re) + jnp.square(lam_im)
    z_re = ((a_re - 1.0) * lam_re + a_im * lam_im) / den
    z_im = (a_im * lam_re - (a_re - 1.0) * lam_im) / den
    bb_re = z_re[..., None] * b_re - z_im[..., None] * b_im
    bb_im = z_re[..., None] * b_im + z_im[..., None] * b_re
    eye = jnp.eye(S5_GROUPS, dtype=F32)
    in_re = jnp.einsum("gph,gk->ghkp", bb_re, eye).reshape(S5_WIDTH, S5_FLAT)
    in_im = jnp.einsum("gph,gk->ghkp", bb_im, eye).reshape(S5_WIDTH, S5_FLAT)
    bblk = jnp.concatenate([in_re, in_im], axis=1).astype(BF16)
    out_re = jnp.einsum("ghp,gk->gpkh", c_re, eye).reshape(S5_FLAT, S5_WIDTH)
    out_im = jnp.einsum("ghp,gk->gpkh", -c_im, eye).reshape(S5_FLAT, S5_WIDTH)
    cblk = jnp.concatenate([out_re, out_im], axis=0).astype(BF16)
    avec = jnp.concatenate([a_re.reshape(1, S5_FLAT), a_im.reshape(1, S5_FLAT)], axis=1)
    return bblk, cblk, d.reshape(1, S5_WIDTH), avec


def _pick_tile(n, pref):
    t = min(n, pref)
    assert n % t == 0
    return t


def kernel(x_prompt, x_sample, cache_k, cache_v, cache_kidx, cache_mem_k, cache_mem_v, state_s5_re, state_s5_im,
           page_table, mem_prompt, w_in, w_a_out, s5_lambda_re, s5_lambda_im, s5_log_dt, s5_b_re, s5_b_im,
           s5_c_re, s5_c_im, s5_d, w_glu_a, w_glu_b, sg_ln_g, sg_ln_b, sg_w, sg_b, w_c_out, w_gate, b_gate,
           w_mix_out, ln1_g, ln1_b, w_mq, w_mk, w_mv, w_mo, ln2_g, ln2_b, w_ff1, w_ff2, ln3_g, ln3_b):
    depth = w_in.shape[0]
    bp, tp, _ = x_prompt.shape
    db, ds, _ = x_sample.shape
    assert ds == 1 and tp % S5_CHUNK == 0 and tp % SG_CHUNK == 0 and tp % KEY_CHUNK == 0
    n_pages = page_table.shape[1]
    past = n_pages * PAGE_SIZE
    alpha = (2 * depth) ** 0.25
    bf = lambda a: a.astype(BF16)
    row = lambda a, l: a[l][None, :]

    w_in_p = _pack_w_in(w_in)
    w_a_out_b, w_glu_a_b, w_glu_b_b, w_c_out_b = bf(w_a_out), bf(w_glu_a), bf(w_glu_b), bf(w_c_out)
    w_gate_b, w_mix_b = bf(w_gate), bf(w_mix_out)
    w_mq_b, w_mk_b, w_mv_b, w_mo_b = bf(w_mq), bf(w_mk), bf(w_mv), bf(w_mo)
    w_ff1_b, w_ff2_b = bf(w_ff1), bf(w_ff2)
    s5m = [_s5_matrices(s5_lambda_re[l], s5_lambda_im[l], s5_log_dt[l], s5_b_re[l], s5_b_im[l], s5_c_re[l],
                        s5_c_im[l], s5_d[l]) for l in range(depth)]

    n = bp * tp
    tm = _pick_tile(tp, 1024)
    cos_p, sin_p = _rope_tables(jnp.arange(tp))
    k_sel_p = min(TOPK_MAX, tp // 4)
    qb = _pick_tile(tp, 512)
    sg_bias = [jnp.repeat(sg_b[l].T, SG_GROUP_CH, axis=1) for l in range(depth)]
    mem2d = mem_prompt.reshape(bp * MEM_TOKENS, D_MODEL)
    h0 = jnp.zeros((bp, 1, 2 * S5_FLAT), F32)

    h = x_prompt.reshape(n, D_MODEL)
    ktp, vtp, iktp, mkp, mvp, s5p = [], [], [], [], [], []
    for l in range(depth):
        bblk, cblk, dvec, avec = s5m[l]
        q, iq, u5, zc, kbf, ikbf, kt, vt, vtbf, ikwt = _proj(h, w_in_p[l], cos_p, sin_p, tp, tm, False)
        r3 = lambda a: a.reshape(bp, tp, a.shape[-1])
        att = _dsa_prompt(r3(q), r3(iq), ikwt, r3(ikbf), r3(kbf), vtbf, k_sel_p, qb)
        u5p = u5.reshape(bp, tp // S5_CHUNK, SUBLANES, S5_SEG, S5_WIDTH).swapaxes(2, 3).reshape(bp, tp, S5_WIDTH)
        y5p, hfin = _s5_prompt(u5p, bblk, cblk, dvec, avec, h0)
        y5 = y5p.reshape(bp, tp // S5_CHUNK, S5_SEG, SUBLANES, S5_WIDTH).swapaxes(2, 3)
        sg = _sgate_prompt(zc, row(sg_ln_g, l), row(sg_ln_b, l), sg_w[l], sg_bias[l], _pick_tile(tp, 512))
        h = _merge(alpha, h, att.reshape(n, A_WIDTH), y5.reshape(n, S5_WIDTH), sg, w_a_out_b[l], w_glu_a_b[l],
                   w_glu_b_b[l], w_c_out_b[l], w_gate_b[l], row(b_gate, l), w_mix_b[l], row(ln1_g, l), row(ln1_b, l), tm)
        mk, mv = _two_linear(mem2d, w_mk_b[l], w_mv_b[l], _pick_tile(bp * MEM_TOKENS, 256))
        h = _mem_prompt(alpha, h.reshape(bp, tp, D_MODEL), mk.reshape(bp, MEM_TOKENS, MEM_WIDTH),
                        mv.reshape(bp, MEM_TOKENS, MEM_WIDTH), w_mq_b[l], w_mo_b[l], row(ln2_g, l), row(ln2_b, l),
                        tm).reshape(n, D_MODEL)
        h = _ffn(alpha, h, w_ff1_b[l], w_ff2_b[l], row(ln3_g, l), row(ln3_b, l), tm)
        ktp.append(kt)
        vtp.append(vt)
        iktp.append(ikwt[:, :IDX_DIM, :])
        mkp.append(mk.reshape(bp, MEM_TOKENS, MEM_HEADS, MEM_HEAD_DIM))
        mvp.append(mv.reshape(bp, MEM_TOKENS, MEM_HEADS, MEM_HEAD_DIM))
        s5p.append(hfin.reshape(bp, 2, S5_GROUPS, S5_STATE))
    y_prompt = h.reshape(bp, tp, D_MODEL)

    cache_k_t = jnp.transpose(cache_k, (0, 1, 3, 4, 2))
    cache_v_t = jnp.transpose(cache_v, (0, 1, 3, 4, 2))
    cache_kidx_t = jnp.swapaxes(cache_kidx, 2, 3)
    cos_s, sin_s = _rope_tables(jnp.full((db,), past, jnp.int32))
    k_sel_s = min(TOPK_MAX, (past + ds) // 4)
    h = x_sample.reshape(db, D_MODEL)
    kts, vts, ikts, s5s, sgs = [], [], [], [], []
    for l in range(depth):
        bblk, cblk, dvec, avec = s5m[l]
        q, iq, u5, zc, _, _, kt, vt, _, ikwt, k, v, ikw = _proj(h, w_in_p[l], cos_s, sin_s, db, db, True)
        head_pad = ((0, 0), (0, SUBLANES - IDX_HEADS), (0, 0))
        scores = _sample_scores(page_table, jnp.pad(iq.reshape(db, IDX_HEADS, IDX_DIM), head_pad),
                                jnp.pad(ikw[:, IDX_DIM:IDX_DIM + IDX_HEADS].reshape(db, IDX_HEADS, 1), head_pad),
                                ikw.reshape(db, 1, LANES), cache_kidx_t, l, _pick_tile(db, 4))
        mask = _sample_select(scores.reshape(db, past + LANES).T, k_sel_s).T
        hd = lambda a: a.reshape(db, A_HEADS, A_HEAD_DIM)
        att = _sample_attend(page_table, hd(q), hd(k), hd(v), mask.reshape(db, 1, past + LANES), cache_k_t, cache_v_t, l)
        h0s = jnp.concatenate([state_s5_re[l].reshape(db, S5_FLAT), state_s5_im[l].reshape(db, S5_FLAT)], axis=1)
        wd = jnp.repeat(sg_w[l][:, 0, 0], SG_GROUP_CH)[None, :]
        bd = jnp.repeat(sg_b[l][:, 0], SG_GROUP_CH)[None, :]
        y5, hnew, sg, vsg = _sample_s5_sg(u5, zc, h0s, bblk, cblk, dvec, avec, row(sg_ln_g, l), row(sg_ln_b, l), wd, bd)
        h = _merge(alpha, h, att.reshape(db, A_WIDTH), y5, sg, w_a_out_b[l], w_glu_a_b[l], w_glu_b_b[l], w_c_out_b[l],
                   w_gate_b[l], row(b_gate, l), w_mix_b[l], row(ln1_g, l), row(ln1_b, l), db)
        mq = _linear(h, w_mq_b[l])
        mo = _mem_sample(mq.reshape(db, MEM_HEADS, MEM_HEAD_DIM), cache_mem_k, cache_mem_v, l, 4)
        h = _linear_res_ln(alpha, mo.reshape(db, MEM_WIDTH), w_mo_b[l], h, row(ln2_g, l), row(ln2_b, l))
        h = _ffn(alpha, h, w_ff1_b[l], w_ff2_b[l], row(ln3_g, l), row(ln3_b, l), db)
        kts.append(kt)
        vts.append(vt)
        ikts.append(ikwt[:, :IDX_DIM, :])
        s5s.append(hnew.reshape(db, 2, S5_GROUPS, S5_STATE))
        sgs.append(vsg.reshape(db, ds, SG_WIDTH))
    y_sample = h.reshape(db, ds, D_MODEL)

    def heads_last(stack, batch, seq):
        return jnp.transpose(stack.reshape(depth, batch, A_HEADS, A_HEAD_DIM, seq), (0, 1, 4, 2, 3))

    s5p, s5s = jnp.stack(s5p), jnp.stack(s5s)
    return (y_prompt, y_sample,
            heads_last(jnp.stack(ktp), bp, tp), heads_last(jnp.stack(vtp), bp, tp), jnp.swapaxes(jnp.stack(iktp), 2, 3),
            jnp.stack(mkp), jnp.stack(mvp), s5p[:, :, 0], s5p[:, :, 1],
            heads_last(jnp.stack(kts), 1, db).reshape(depth, db, ds, A_HEADS, A_HEAD_DIM),
            heads_last(jnp.stack(vts), 1, db).reshape(depth, db, ds, A_HEADS, A_HEAD_DIM),
            jnp.swapaxes(jnp.stack(ikts), 2, 3).reshape(depth, db, ds, IDX_DIM),
            s5s[:, :, 0], s5s[:, :, 1], jnp.stack(sgs))
```

```python
import functools
import math

import numpy as np
import jax
import jax.numpy as jnp
from jax import lax
from jax.experimental import pallas as pl
from jax.experimental.pallas import tpu as pltpu

F32 = jnp.float32
BF16 = jnp.bfloat16
I32 = jnp.int32

D_MODEL = 1024
A_HEADS = 8
A_HEAD_DIM = 64
A_WIDTH = A_HEADS * A_HEAD_DIM
IDX_HEADS = 4
IDX_DIM = 64
TOPK_MAX = 256
ROPE_THETA = 10000.0
S5_GROUP_CH = 16
S5_GROUPS = 16
S5_WIDTH = S5_GROUPS * S5_GROUP_CH
S5_STATE = 64
S5_FLAT = S5_GROUPS * S5_STATE
SG_CHUNK = 128
SG_GROUPS = 4
SG_WIDTH = 256
SG_GROUP_CH = SG_WIDTH // SG_GROUPS
MEM_TOKENS = 256
MEM_HEADS = 4
MEM_HEAD_DIM = 128
MEM_WIDTH = MEM_HEADS * MEM_HEAD_DIM
D_FF = 4 * D_MODEL
N_BRANCH = 3
LN_EPS = 1e-5
PAGE_SIZE = 128

LANES = 128
SUBLANES = 8
VMEM_PHYSICAL_BYTES = 64 * 1024 * 1024

IKW_PAD = LANES - IDX_DIM - IDX_HEADS
OFF_Q = 0
OFF_K = OFF_Q + A_WIDTH
OFF_V = OFF_K + A_WIDTH
OFF_IQ = OFF_V + A_WIDTH
OFF_IKW = OFF_IQ + IDX_HEADS * IDX_DIM
OFF_U5 = OFF_IKW + LANES
OFF_ZC = OFF_U5 + S5_WIDTH
IN_PACKED = OFF_ZC + 2 * SG_WIDTH

INT_MIN = -2 ** 31
LOG2_E = math.log2(math.e)
NEG_INF_KEY = int(np.array([0xFF800000 ^ 0x7FFFFFFF], dtype=np.uint32).view(np.int32)[0])


def _cparams(semantics, vmem_bytes):
    assert vmem_bytes < VMEM_PHYSICAL_BYTES
    return pltpu.CompilerParams(dimension_semantics=semantics, vmem_limit_bytes=int(vmem_bytes))


def _mib(n):
    return n * 1024 * 1024


def _const_spec(shape):
    n = len(shape)
    return pl.BlockSpec(shape, lambda *_: (0,) * n, pipeline_mode=pl.Buffered(1))


def _layer_norm(x, g, b):
    mu = jnp.mean(x, axis=-1, keepdims=True)
    xc = x - mu
    var = jnp.mean(xc * xc, axis=-1, keepdims=True)
    return xc * lax.rsqrt(var + LN_EPS) * g + b


def _dot(a, b):
    return jnp.dot(a.astype(BF16), b.astype(BF16), preferred_element_type=F32)


def _dot_nt(a, b):
    return lax.dot_general(a.astype(BF16), b.astype(BF16), (((1,), (1,)), ((), ())),
                           preferred_element_type=F32)


def _proj_kernel(emit_rows, n_carried, x_ref, w_ref, cos_ref, sin_ref, *refs):
    q_ref, iq_ref, u5_ref, zc_ref, kbf_ref, ikbf_ref, kt_ref, vt_ref, vtbf_ref, ikwt_ref = refs[n_carried:n_carried + 10]
    row_refs = refs[n_carried + 10:]
    y = _dot(x_ref[...], w_ref[...])
    cos = cos_ref[...]
    sin = sin_ref[...]
    lane = lax.broadcasted_iota(I32, cos.shape, 1)
    first_half = (lane & (A_HEAD_DIM // 2)) == 0

    def rope(t, c, s):
        partner = jnp.where(first_half, pltpu.roll(t, LANES - A_HEAD_DIM // 2, 1), pltpu.roll(t, A_HEAD_DIM // 2, 1))
        return t * c + partner * s

    for g in range(A_WIDTH // LANES):
        sl = slice(g * LANES, (g + 1) * LANES)
        q_ref[:, sl] = rope(y[:, OFF_Q + g * LANES:OFF_Q + (g + 1) * LANES], cos, sin)
        kg = rope(y[:, OFF_K + g * LANES:OFF_K + (g + 1) * LANES], cos, sin)
        kbf_ref[:, sl] = kg.astype(BF16)
        kt_ref[0, 0, sl, :] = kg.T
        vg = y[:, OFF_V + g * LANES:OFF_V + (g + 1) * LANES]
        vt = vg.T
        vt_ref[0, 0, sl, :] = vt
        vtbf_ref[0, sl, :] = vt.astype(BF16)
        if emit_rows:
            row_refs[0][:, sl] = kg
            row_refs[1][:, sl] = vg
    for g in range(IDX_HEADS * IDX_DIM // LANES):
        iq_ref[:, g * LANES:(g + 1) * LANES] = rope(y[:, OFF_IQ + g * LANES:OFF_IQ + (g + 1) * LANES], cos, sin)
    is_key = lane < IDX_DIM
    ikw = rope(y[:, OFF_IKW:OFF_IKW + LANES], jnp.where(is_key, cos, 1.0), jnp.where(is_key, sin, 0.0))
    ikbf_ref[...] = ikw[:, :IDX_DIM].astype(BF16)
    ikwt_ref[0] = ikw.T
    if emit_rows:
        row_refs[2][...] = ikw
    u5_ref[...] = y[:, OFF_U5:OFF_U5 + S5_WIDTH]
    zc_ref[...] = y[:, OFF_ZC:OFF_ZC + 2 * SG_WIDTH]


def _proj(x, w, cos, sin, seq, tm, emit_rows, layer, depth, carried):
    n = x.shape[0]
    nt = seq // tm
    row = lambda w_: pl.BlockSpec((tm, w_), lambda i: (i, 0))
    tab = pl.BlockSpec((tm, LANES), lambda i: (i % nt, 0))
    tr = lambda w_: pl.BlockSpec((1, w_, tm), lambda i: (i // nt, 0, i % nt))
    sds = jax.ShapeDtypeStruct
    row_out = [(A_WIDTH, F32), (IDX_HEADS * IDX_DIM, F32), (S5_WIDTH, F32), (2 * SG_WIDTH, F32), (A_WIDTH, BF16),
               (IDX_DIM, BF16)]
    stacked = pl.BlockSpec((1, 1, A_WIDTH, tm), lambda i: (layer, i // nt, 0, i % nt))
    tr_out = [(A_WIDTH, BF16), (LANES, F32)]
    extra = [(A_WIDTH, F32), (A_WIDTH, F32), (LANES, F32)] if emit_rows else []
    out_shape = ([sds((n, w_), dt) for w_, dt in row_out] + [sds((depth, n // seq, A_WIDTH, seq), F32)] * 2
                 + [sds((n // seq, w_, seq), dt) for w_, dt in tr_out] + [sds((n, w_), dt) for w_, dt in extra])
    out_specs = ([row(w_) for w_, _ in row_out] + [stacked] * 2 + [tr(w_) for w_, _ in tr_out]
                 + [row(w_) for w_, _ in extra])
    carried = () if carried is None else tuple(carried)
    n_fixed = 4
    return pl.pallas_call(
        functools.partial(_proj_kernel, emit_rows, len(carried)), grid=(n // tm,),
        in_specs=[row(D_MODEL), _const_spec((D_MODEL, IN_PACKED)), tab, tab] + [pl.BlockSpec(memory_space=pl.ANY)] * len(carried),
        out_specs=out_specs, out_shape=out_shape,
        input_output_aliases={n_fixed + j: len(row_out) + j for j in range(len(carried))},
        compiler_params=_cparams(("parallel",), _mib(40)), name="in_proj_rope",
    )(x, w, cos, sin, *carried)


KEY_CHUNK = 512


def _score_keys(score):
    score = jnp.where(score == 0.0, 0.0, score)
    bits = lax.bitcast_convert_type(score, I32)
    return jnp.where(bits < 0, bits ^ 0x7FFFFFFF, bits)


def _rows_reduce(v, op):
    accs = [None] * 4
    for i in range(v.shape[0] // SUBLANES):
        part = v[i * SUBLANES:(i + 1) * SUBLANES]
        accs[i % 4] = part if accs[i % 4] is None else op(accs[i % 4], part)
    return op(op(accs[0], accs[1]), op(accs[2], accs[3]))


def _chunk_rows(c):
    return pl.ds(pl.multiple_of(c * KEY_CHUNK, KEY_CHUNK), KEY_CHUNK)


def _count(key_ref, n_chunks, pred, extra=None):
    r = key_ref.shape[1]
    tile = SUBLANES * (4 // key_ref.dtype.itemsize)

    def body(c, acc):
        ch = key_ref[_chunk_rows(c), :]
        accs = [acc, None, None, None]
        for i in range(KEY_CHUNK // tile):
            v = pred(ch[i * tile:(i + 1) * tile], c * KEY_CHUNK + i * tile)
            accs[i % 4] = v if accs[i % 4] is None else accs[i % 4] + v
        return (accs[0] + accs[1]) + (accs[2] + accs[3])

    acc = lax.fori_loop(0, n_chunks, body, jnp.zeros((tile, r), key_ref.dtype))
    cnt = jnp.sum(acc.astype(I32), axis=0, keepdims=True)
    return cnt if extra is None else cnt + extra


def _as_int(mask):
    return jnp.where(mask, 1, 0)


def _as_i16(mask):
    return jnp.where(mask, jnp.int16(1), jnp.int16(0))


HALF = 1 << 15


def _search16(half_ref, n_chunks, need, xhalf=None):
    r = half_ref.shape[1]

    def bit_body(i, prefix):
        cand = (prefix | jnp.left_shift(jnp.int32(1), 15 - i)) - HALF
        cand16 = cand.astype(jnp.int16)
        cnt = _count(half_ref, n_chunks, lambda ch, _: _as_i16(ch >= cand16),
                     None if xhalf is None else _as_int(xhalf >= cand))
        return jnp.where(cnt >= need, cand + HALF, prefix)

    return lax.fori_loop(0, 16, bit_body, jnp.zeros((1, r), I32)) - HALF


def _topk_threshold(key_ref, half_ref, n_chunks, k_sel, last_row, xkey=None):
    r = key_ref.shape[1]

    def split(k):
        return jnp.right_shift(k, 16), (k & 0xFFFF) - HALF

    def fill_hi(c, carry):
        half_ref[_chunk_rows(c), :] = split(key_ref[_chunk_rows(c), :])[0].astype(jnp.int16)
        return carry

    lax.fori_loop(0, n_chunks, fill_hi, 0)
    xhi, xlo = (None, None) if xkey is None else split(xkey)
    t_hi = _search16(half_ref, n_chunks, k_sel, xhi)
    t_hi16 = t_hi.astype(jnp.int16)
    above = _count(half_ref, n_chunks, lambda ch, _: _as_i16(ch > t_hi16), None if xkey is None else _as_int(xhi > t_hi))

    def fill_lo(c, carry):
        hi, lo = split(key_ref[_chunk_rows(c), :])
        half_ref[_chunk_rows(c), :] = jnp.where(hi == t_hi, lo, -HALF).astype(jnp.int16)
        return carry

    lax.fori_loop(0, n_chunks, fill_lo, 0)
    t_lo = _search16(half_ref, n_chunks, k_sel - above, None if xkey is None else jnp.where(xhi == t_hi, xlo, -HALF))
    thr = t_hi * (2 * HALF) + (t_lo + HALF)

    cnt_gt = _count(key_ref, n_chunks, lambda ch, _: _as_int(ch > thr), None if xkey is None else _as_int(xkey > thr))
    cnt_eq = _count(key_ref, n_chunks, lambda ch, _: _as_int(ch == thr), None if xkey is None else _as_int(xkey == thr))
    need = k_sel - cnt_gt
    surplus = jnp.where(cnt_eq > need, _as_int(thr > NEG_INF_KEY), 0)
    any_surplus = jnp.max(surplus) > 0
    row = lax.broadcasted_iota(I32, (SUBLANES, r), 0)
    n_bits = int(last_row).bit_length()

    def tie_search():
        def body(i, lo):
            cand = lo + jnp.left_shift(jnp.int32(1), n_bits - 1 - i)
            extra = None if xkey is None else jnp.where(xkey == thr, _as_int(cand >= last_row), 0)
            cnt = _count(key_ref, n_chunks, lambda ch, r0: jnp.where(ch == thr, _as_int(row + r0 <= cand), 0), extra)
            return jnp.where(cnt < need, cand, lo)

        return lax.fori_loop(0, n_bits, body, jnp.full((1, r), -1, I32)) + 1

    jstar = lax.cond(any_surplus, tie_search, lambda: jnp.full((1, r), last_row, I32))
    return thr, jstar


def _selected(ch, row, thr, jstar):
    return jnp.where(ch == thr, _as_int(row <= jstar), _as_int(ch > thr)) > 0


def _dsa_prompt_kernel(k_sel, q_ref, iq_ref, iwt_ref, ikbf_ref, kbf_ref, vtbf_ref, o_ref, key_scr, half_scr, bias_scr,
                       qpair_scr, m_scr, l_scr, acc_scr):
    qb = q_ref.shape[1]
    t = kbf_ref.shape[1]
    q0 = pl.program_id(1) * qb
    n_chunks = (q0 + qb + KEY_CHUNK - 1) // KEY_CHUNK
    qcol = q0 + lax.broadcasted_iota(I32, (1, qb), 1)
    row = lax.broadcasted_iota(I32, (KEY_CHUNK, qb), 0)
    iwt = iwt_ref[0][:IDX_HEADS] * (IDX_HEADS ** -0.5)
    iq = iq_ref[0] * (IDX_DIM ** -0.5)
    iqh = [iq[:, h * IDX_DIM:(h + 1) * IDX_DIM].astype(BF16) for h in range(IDX_HEADS)]

    def score_chunk(c, carry):
        ik = ikbf_ref[0, _chunk_rows(c), :]
        score = jnp.zeros((KEY_CHUNK, qb), F32)
        for h in range(IDX_HEADS):
            score = score + iwt[h:h + 1, :] * jnp.maximum(_dot_nt(ik, iqh[h]), 0.0)
        key_scr[_chunk_rows(c), :] = _score_keys(jnp.where(row + c * KEY_CHUNK <= qcol, score, -jnp.inf))
        return carry

    lax.fori_loop(0, n_chunks, score_chunk, 0)
    thr, jstar = _topk_threshold(key_scr, half_scr, n_chunks, k_sel, t - 1)

    def bias_chunk(c, carry):
        krow = row + c * KEY_CHUNK
        sel = _selected(key_scr[_chunk_rows(c), :], krow, thr, jstar)
        bias_scr[_chunk_rows(c), :] = jnp.where(krow <= qcol, jnp.where(sel, 0.0, -jnp.inf), -jnp.inf)
        return carry

    lax.fori_loop(0, n_chunks, bias_chunk, 0)

    qs = q_ref[0] * (A_HEAD_DIM ** -0.5 * LOG2_E)
    lane = lax.broadcasted_iota(I32, (qb, LANES), 1)
    for g in range(A_WIDTH // LANES):
        qg = qs[:, g * LANES:(g + 1) * LANES]
        qpair_scr[g, :qb, :] = jnp.where(lane < A_HEAD_DIM, qg, 0.0).astype(BF16)
        qpair_scr[g, qb:, :] = jnp.where(lane < A_HEAD_DIM, 0.0, qg).astype(BF16)
    m_scr[...] = jnp.full(m_scr.shape, -jnp.inf, F32)
    l_scr[...] = jnp.zeros(l_scr.shape, F32)
    acc_scr[...] = jnp.zeros(acc_scr.shape, F32)

    def attend_chunk(c, carry):
        rows = _chunk_rows(c)
        bias = bias_scr[rows, :]
        for g in range(A_WIDTH // LANES):
            s_pair = _dot_nt(kbf_ref[0, rows, g * LANES:(g + 1) * LANES], qpair_scr[g])
            for half in range(2):
                h = 2 * g + half
                s = s_pair[:, half * qb:(half + 1) * qb] + bias
                m_old = m_scr[h:h + 1, :]
                m_new = jnp.maximum(m_old, jnp.max(_rows_reduce(s, jnp.maximum), axis=0, keepdims=True))
                m_use = jnp.where(m_new == -jnp.inf, 0.0, m_new)
                p = jnp.exp2(s - m_use)
                resc = jnp.exp2(m_old - m_use)
                m_scr[h:h + 1, :] = m_new
                l_scr[h:h + 1, :] = resc * l_scr[h:h + 1, :] + jnp.sum(_rows_reduce(p, jnp.add), axis=0, keepdims=True)
                hd = slice(h * A_HEAD_DIM, (h + 1) * A_HEAD_DIM)
                acc_scr[hd, :] = resc * acc_scr[hd, :] + _dot(vtbf_ref[0, hd, rows], p)
        return carry

    lax.fori_loop(0, n_chunks, attend_chunk, 0)
    out_row = lax.broadcasted_iota(I32, (LANES, qb), 0)
    for g in range(A_WIDTH // LANES):
        l_pair = jnp.where(out_row < A_HEAD_DIM, l_scr[2 * g:2 * g + 1, :], l_scr[2 * g + 1:2 * g + 2, :])
        o_ref[0, :, g * LANES:(g + 1) * LANES] = (acc_scr[g * LANES:(g + 1) * LANES, :] / l_pair).T


def _dsa_prompt(q, iq, ikwt, ikbf, kbf, vtbf, k_sel, qb):
    b, t, _ = q.shape
    blk = lambda w_: pl.BlockSpec((1, qb, w_), lambda i, j: (i, j, 0))
    full = lambda w_: pl.BlockSpec((1, t, w_), lambda i, j: (i, 0, 0))
    return pl.pallas_call(
        functools.partial(_dsa_prompt_kernel, k_sel), grid=(b, t // qb),
        in_specs=[blk(A_WIDTH), blk(IDX_HEADS * IDX_DIM),
                  pl.BlockSpec((1, SUBLANES, qb), lambda i, j: (i, IDX_DIM // SUBLANES, j)),
                  full(IDX_DIM), full(A_WIDTH), pl.BlockSpec((1, A_WIDTH, t), lambda i, j: (i, 0, 0))],
        out_specs=blk(A_WIDTH), out_shape=jax.ShapeDtypeStruct((b, t, A_WIDTH), F32),
        scratch_shapes=[pltpu.VMEM((t, qb), I32), pltpu.VMEM((t, qb), jnp.int16), pltpu.VMEM((t, qb), F32),
                        pltpu.VMEM((A_WIDTH // LANES, 2 * qb, LANES), BF16), pltpu.VMEM((A_HEADS, qb), F32),
                        pltpu.VMEM((A_HEADS, qb), F32), pltpu.VMEM((A_WIDTH, qb), F32)],
        compiler_params=_cparams(("parallel", "arbitrary"), _mib(48)), name="dsa_prompt",
    )(q, iq, ikwt, ikbf, kbf, vtbf)


S5_SEG = 64
S5_CHUNK = S5_SEG * SUBLANES


def _s5_prompt_kernel(u_ref, bblk_ref, cblk_ref, d_ref, a_ref, h0_ref, y_ref, hout_ref,
                      h_scr, pw_scr, hin_scr, carry_scr):
    c = pl.program_id(1)
    a = a_ref[...]
    a_re1, a_im1 = a[:, :S5_FLAT], a[:, S5_FLAT:]

    @pl.when(c == 0)
    def _():
        carry_scr[...] = h0_ref[0]
        pr, pi = a_re1, a_im1
        for k in range(S5_SEG):
            pw_scr[k:k + 1, :S5_FLAT] = pr
            pw_scr[k:k + 1, S5_FLAT:] = pi
            pr, pi = pr * a_re1 - pi * a_im1, pr * a_im1 + pi * a_re1

    up = u_ref[0]
    h_scr[...] = _dot(up, bblk_ref[...])

    a_re = jnp.broadcast_to(a_re1, (SUBLANES, S5_FLAT))
    a_im = jnp.broadcast_to(a_im1, (SUBLANES, S5_FLAT))

    def scan_step(k, carry):
        hr, hi = carry
        rows = pl.ds(pl.multiple_of(k * SUBLANES, SUBLANES), SUBLANES)
        nhr = a_re * hr - a_im * hi + h_scr[rows, :S5_FLAT]
        nhi = a_re * hi + a_im * hr + h_scr[rows, S5_FLAT:]
        h_scr[rows, :S5_FLAT] = nhr
        h_scr[rows, S5_FLAT:] = nhi
        return nhr, nhi

    zero = jnp.zeros((SUBLANES, S5_FLAT), F32)
    end_re, end_im = lax.fori_loop(0, S5_SEG, scan_step, (zero, zero))

    p_last_re, p_last_im = pw_scr[S5_SEG - 1:S5_SEG, :S5_FLAT], pw_scr[S5_SEG - 1:S5_SEG, S5_FLAT:]
    cr, ci = carry_scr[:, :S5_FLAT], carry_scr[:, S5_FLAT:]
    for j in range(SUBLANES):
        hin_scr[j:j + 1, :S5_FLAT] = cr
        hin_scr[j:j + 1, S5_FLAT:] = ci
        cr, ci = (end_re[j:j + 1] + p_last_re * cr - p_last_im * ci,
                  end_im[j:j + 1] + p_last_re * ci + p_last_im * cr)
    carry_scr[:, :S5_FLAT] = cr
    carry_scr[:, S5_FLAT:] = ci
    hin_re, hin_im = hin_scr[:, :S5_FLAT], hin_scr[:, S5_FLAT:]

    def fix_step(k, _):
        rows = pl.ds(pl.multiple_of(k * SUBLANES, SUBLANES), SUBLANES)
        pr = jnp.broadcast_to(pw_scr[pl.ds(k, 1), :S5_FLAT], (SUBLANES, S5_FLAT))
        pi = jnp.broadcast_to(pw_scr[pl.ds(k, 1), S5_FLAT:], (SUBLANES, S5_FLAT))
        h_scr[rows, :S5_FLAT] = h_scr[rows, :S5_FLAT] + pr * hin_re - pi * hin_im
        h_scr[rows, S5_FLAT:] = h_scr[rows, S5_FLAT:] + pr * hin_im + pi * hin_re
        return 0

    lax.fori_loop(0, S5_SEG, fix_step, 0)
    y_ref[0] = _dot(h_scr[...], cblk_ref[...]) + d_ref[...] * up

    @pl.when(c == pl.num_programs(1) - 1)
    def _():
        hout_ref[0] = carry_scr[...]


def _s5_prompt(u, bblk, cblk, dvec, avec, h0):
    b, t, _ = u.shape
    blk = pl.BlockSpec((1, S5_CHUNK, S5_WIDTH), lambda i, j: (i, j, 0))
    st = pl.BlockSpec((1, 1, 2 * S5_FLAT), lambda i, j: (i, 0, 0))
    return pl.pallas_call(
        _s5_prompt_kernel, grid=(b, t // S5_CHUNK),
        in_specs=[blk, _const_spec((S5_WIDTH, 2 * S5_FLAT)), _const_spec((2 * S5_FLAT, S5_WIDTH)),
                  _const_spec((1, S5_WIDTH)), _const_spec((1, 2 * S5_FLAT)), st],
        out_specs=[blk, st],
        out_shape=[jax.ShapeDtypeStruct((b, t, S5_WIDTH), F32), jax.ShapeDtypeStruct((b, 1, 2 * S5_FLAT), F32)],
        scratch_shapes=[pltpu.VMEM((S5_CHUNK, 2 * S5_FLAT), F32),
                        pltpu.VMEM((S5_SEG, 2 * S5_FLAT), F32), pltpu.VMEM((SUBLANES, 2 * S5_FLAT), F32),
                        pltpu.VMEM((1, 2 * S5_FLAT), F32)],
        compiler_params=_cparams(("parallel", "arbitrary"), _mib(32)), name="s5_prompt",
    )(u, bblk, cblk, dvec, avec, h0)


def _sgate_prompt_kernel(zc_ref, g_ref, b_ref, w_ref, bias_ref, o_ref):
    z = jax.nn.gelu(zc_ref[...])
    u = z[:, :SG_WIDTH]
    v = _layer_norm(z[:, SG_WIDTH:], g_ref[...], b_ref[...])
    ri = lax.broadcasted_iota(I32, (SG_CHUNK, SG_CHUNK), 0)
    ci = lax.broadcasted_iota(I32, (SG_CHUNK, SG_CHUNK), 1)
    wm = [jnp.where(ci <= ri, w_ref[g], 0.0).astype(BF16) for g in range(SG_GROUPS)]
    low = lax.broadcasted_iota(I32, (SG_CHUNK, LANES), 1) < SG_GROUP_CH
    bias = bias_ref[...]
    for c in range(zc_ref.shape[0] // SG_CHUNK):
        rows = slice(c * SG_CHUNK, (c + 1) * SG_CHUNK)
        for pr in range(SG_WIDTH // LANES):
            cols = slice(pr * LANES, (pr + 1) * LANES)
            vp = v[rows, cols].astype(BF16)
            s = jnp.where(low, _dot(wm[2 * pr], vp), _dot(wm[2 * pr + 1], vp)) + bias[:, cols]
            o_ref[rows, cols] = u[rows, cols] * s


def _sgate_prompt(zc, g, b, w, bias, tm):
    n = zc.shape[0]
    return pl.pallas_call(
        _sgate_prompt_kernel, grid=(n // tm,),
        in_specs=[pl.BlockSpec((tm, 2 * SG_WIDTH), lambda i: (i, 0)), _const_spec((1, SG_WIDTH)),
                  _const_spec((1, SG_WIDTH)), _const_spec((SG_GROUPS, SG_CHUNK, SG_CHUNK)),
                  _const_spec((SG_CHUNK, SG_WIDTH))],
        out_specs=pl.BlockSpec((tm, SG_WIDTH), lambda i: (i, 0)),
        out_shape=jax.ShapeDtypeStruct((n, SG_WIDTH), F32),
        compiler_params=_cparams(("parallel",), _mib(24)), name="sgate_prompt",
    )(zc, g, b, w, bias)


def _merge_kernel(alpha, x_ref, att_ref, y5_ref, sg_ref, wa_ref, wga_ref, wgb_ref, wc_ref, wg_ref, bg_ref, wmix_ref,
                  g_ref, b_ref, o_ref):
    x = x_ref[...]
    gates = jax.nn.sigmoid(_dot(x, wg_ref[...]) + bg_ref[...])
    br_a = _dot(att_ref[...], wa_ref[...])
    y5 = jax.nn.gelu(y5_ref[...])
    br_b = _dot(y5, wga_ref[...]) * jax.nn.sigmoid(_dot(y5, wgb_ref[...]))
    br_c = _dot(sg_ref[...], wc_ref[...])
    merged = (gates[:, :D_MODEL] * br_a + gates[:, D_MODEL:2 * D_MODEL] * br_b + gates[:, 2 * D_MODEL:] * br_c)
    o_ref[...] = _layer_norm(alpha * x + _dot(merged, wmix_ref[...]), g_ref[...], b_ref[...])


def _merge(alpha, x, att, y5, sg, wa, wga, wgb, wc, wg, bg, wmix, g, b, tm):
    n = x.shape[0]
    row = lambda w_: pl.BlockSpec((tm, w_), lambda i: (i, 0))
    return pl.pallas_call(
        functools.partial(_merge_kernel, alpha), grid=(n // tm,),
        in_specs=[row(D_MODEL), row(A_WIDTH), row(S5_WIDTH), row(SG_WIDTH),
                  _const_spec((A_WIDTH, D_MODEL)), _const_spec((S5_WIDTH, D_MODEL)), _const_spec((S5_WIDTH, D_MODEL)),
                  _const_spec((SG_WIDTH, D_MODEL)), _const_spec((D_MODEL, N_BRANCH * D_MODEL)),
                  _const_spec((1, N_BRANCH * D_MODEL)), _const_spec((D_MODEL, D_MODEL)),
                  _const_spec((1, D_MODEL)), _const_spec((1, D_MODEL))],
        out_specs=row(D_MODEL), out_shape=jax.ShapeDtypeStruct((n, D_MODEL), F32),
        compiler_params=_cparams(("parallel",), _mib(48)), name="branch_merge",
    )(x, att, y5, sg, wa, wga, wgb, wc, wg, bg, wmix, g, b)


def _two_linear_kernel(x_ref, w1_ref, w2_ref, o1_ref, o2_ref):
    x = x_ref[...]
    o1_ref[...] = _dot(x, w1_ref[...])
    o2_ref[...] = _dot(x, w2_ref[...])


def _two_linear(x, w1, w2, tm):
    n, k = x.shape
    m = w1.shape[1]
    return pl.pallas_call(
        _two_linear_kernel, grid=(n // tm,),
        in_specs=[pl.BlockSpec((tm, k), lambda i: (i, 0)), _const_spec((k, m)), _const_spec((k, m))],
        out_specs=[pl.BlockSpec((tm, m), lambda i: (i, 0))] * 2,
        out_shape=[jax.ShapeDtypeStruct((n, m), F32)] * 2,
        compiler_params=_cparams(("parallel",), _mib(24)), name="mem_kv_proj",
    )(x, w1, w2)


def _mem_prompt_kernel(alpha, x_ref, mk_ref, mv_ref, wq_ref, wo_ref, g_ref, b_ref, o_ref):
    x = x_ref[0]
    q = _dot(x, wq_ref[...])
    mk = mk_ref[0].astype(BF16)
    mv = mv_ref[0].astype(BF16)
    outs = []
    for h in range(MEM_HEADS):
        sl = slice(h * MEM_HEAD_DIM, (h + 1) * MEM_HEAD_DIM)
        s = _dot_nt(q[:, sl], mk[:, sl]) * (MEM_HEAD_DIM ** -0.5)
        m = jnp.max(s, axis=1, keepdims=True)
        p = jnp.exp(s - m)
        l = jnp.sum(p, axis=1, keepdims=True)
        outs.append(_dot(p, mv[:, sl]) / l)
    o = jnp.concatenate(outs, axis=1)
    o_ref[0] = _layer_norm(alpha * x + _dot(o, wo_ref[...]), g_ref[...], b_ref[...])


def _mem_prompt(alpha, x, mk, mv, wq, wo, g, b, tm):
    bsz, t, _ = x.shape
    return pl.pallas_call(
        functools.partial(_mem_prompt_kernel, alpha), grid=(bsz, t // tm),
        in_specs=[pl.BlockSpec((1, tm, D_MODEL), lambda i, j: (i, j, 0)),
                  pl.BlockSpec((1, MEM_TOKENS, MEM_WIDTH), lambda i, j: (i, 0, 0)),
                  pl.BlockSpec((1, MEM_TOKENS, MEM_WIDTH), lambda i, j: (i, 0, 0)),
                  _const_spec((D_MODEL, MEM_WIDTH)), _const_spec((MEM_WIDTH, D_MODEL)),
                  _const_spec((1, D_MODEL)), _const_spec((1, D_MODEL))],
        out_specs=pl.BlockSpec((1, tm, D_MODEL), lambda i, j: (i, j, 0)),
        out_shape=jax.ShapeDtypeStruct((bsz, t, D_MODEL), F32),
        compiler_params=_cparams(("parallel", "parallel"), _mib(32)), name="mem_attend_prompt",
    )(x, mk, mv, wq, wo, g, b)


def _ffn_kernel(alpha, x_ref, w1_ref, w2_ref, g_ref, b_ref, o_ref):
    x = x_ref[...]
    hid = jnp.square(jnp.maximum(_dot(x, w1_ref[...]), 0.0))
    o_ref[...] = _layer_norm(alpha * x + _dot(hid, w2_ref[...]), g_ref[...], b_ref[...])


def _ffn(alpha, x, w1, w2, g, b, tm):
    n = x.shape[0]
    row = pl.BlockSpec((tm, D_MODEL), lambda i: (i, 0))
    return pl.pallas_call(
        functools.partial(_ffn_kernel, alpha), grid=(n // tm,),
        in_specs=[row, _const_spec((D_MODEL, D_FF)), _const_spec((D_FF, D_MODEL)),
                  _const_spec((1, D_MODEL)), _const_spec((1, D_MODEL))],
        out_specs=row, out_shape=jax.ShapeDtypeStruct((n, D_MODEL), F32),
        compiler_params=_cparams(("parallel",), _mib(56)), name="relu2_mlp",
    )(x, w1, w2, g, b)


def _sample_s5_sg_kernel(u_ref, zc_ref, h0_ref, bblk_ref, cblk_ref, d_ref, a_ref, g_ref, b_ref, wd_ref, bd_ref,
                         y_ref, h_ref, sg_ref, vsg_ref):
    u = u_ref[...]
    a = a_ref[...]
    a_re, a_im = a[:, :S5_FLAT], a[:, S5_FLAT:]
    h0 = h0_ref[...]
    h0_re, h0_im = h0[:, :S5_FLAT], h0[:, S5_FLAT:]
    bu = _dot(u, bblk_ref[...])
    h_re = bu[:, :S5_FLAT] + (a_re * h0_re - a_im * h0_im)
    h_im = bu[:, S5_FLAT:] + (a_re * h0_im + a_im * h0_re)
    h_ref[:, :S5_FLAT] = h_re
    h_ref[:, S5_FLAT:] = h_im
    y_ref[...] = _dot(h_ref[...], cblk_ref[...]) + d_ref[...] * u
    z = jax.nn.gelu(zc_ref[...])
    v = _layer_norm(z[:, SG_WIDTH:], g_ref[...], b_ref[...])
    vsg_ref[...] = v
    sg_ref[...] = z[:, :SG_WIDTH] * (wd_ref[...] * v + bd_ref[...])


def _sample_s5_sg(u, zc, h0, bblk, cblk, dvec, avec, g, b, wd, bd):
    n = u.shape[0]
    full = lambda a: _const_spec(a.shape)
    args = (u, zc, h0, bblk, cblk, dvec, avec, g, b, wd, bd)
    return pl.pallas_call(
        _sample_s5_sg_kernel, grid=(1,), in_specs=[full(a) for a in args],
        out_specs=[_const_spec((n, S5_WIDTH)), _const_spec((n, 2 * S5_FLAT)), _const_spec((n, SG_WIDTH)),
                   _const_spec((n, SG_WIDTH))],
        out_shape=[jax.ShapeDtypeStruct((n, S5_WIDTH), F32), jax.ShapeDtypeStruct((n, 2 * S5_FLAT), F32),
                   jax.ShapeDtypeStruct((n, SG_WIDTH), F32), jax.ShapeDtypeStruct((n, SG_WIDTH), F32)],
        compiler_params=_cparams(("arbitrary",), _mib(24)), name="sample_s5_sgate",
    )(*args)


def _sample_scores_kernel(n_pages, group, pt_ref, iq_ref, iw_ref, ikw_ref, *refs):
    page_refs, o_ref = refs[:group * n_pages], refs[group * n_pages]
    for g in range(group):
        iq = iq_ref[g]
        iw = iw_ref[g] * (IDX_HEADS ** -0.5)
        for j in range(n_pages):
            dots = _dot(iq, page_refs[g * n_pages + j][0, 0]) * (IDX_DIM ** -0.5)
            o_ref[g, :, j * PAGE_SIZE:(j + 1) * PAGE_SIZE] = jnp.sum(iw * jnp.maximum(dots, 0.0), axis=0, keepdims=True)
        own = jnp.sum(iq * ikw_ref[g][:, :IDX_DIM], axis=1, keepdims=True) * (IDX_DIM ** -0.5)
        own = jnp.sum(iw * jnp.maximum(own, 0.0), axis=0, keepdims=True)
        o_ref[g, :, n_pages * PAGE_SIZE:] = jnp.broadcast_to(own, (1, LANES))


def _sample_scores(page_table, iq, iw, ikw, cache_kidx_t, layer, group):
    n, n_pages = page_table.shape
    width = n_pages * PAGE_SIZE + LANES
    page = lambda g, j: pl.BlockSpec((1, 1, IDX_DIM, PAGE_SIZE), lambda i, pt: (layer, pt[i * group + g, j], 0, 0))
    grid_spec = pltpu.PrefetchScalarGridSpec(
        num_scalar_prefetch=1, grid=(n // group,),
        in_specs=[pl.BlockSpec((group, SUBLANES, IDX_DIM), lambda i, pt: (i, 0, 0)),
                  pl.BlockSpec((group, SUBLANES, 1), lambda i, pt: (i, 0, 0)),
                  pl.BlockSpec((group, 1, LANES), lambda i, pt: (i, 0, 0))]
        + [page(g, j) for g in range(group) for j in range(n_pages)],
        out_specs=pl.BlockSpec((group, 1, width), lambda i, pt: (i, 0, 0)))
    return pl.pallas_call(
        functools.partial(_sample_scores_kernel, n_pages, group), grid_spec=grid_spec,
        out_shape=jax.ShapeDtypeStruct((n, 1, width), F32),
        compiler_params=_cparams(("arbitrary",), _mib(24)), name="sample_idx_scores",
    )(page_table, iq, iw, ikw, *([cache_kidx_t] * (group * n_pages)))


def _sample_select_kernel(k_sel, sct_ref, o_ref, key_scr, half_scr):
    past, r = key_scr.shape
    n_chunks = past // KEY_CHUNK
    key_scr[...] = _score_keys(sct_ref[:past, :])
    xkey = _score_keys(sct_ref[past:past + 1, :])
    thr, jstar = _topk_threshold(key_scr, half_scr, n_chunks, k_sel, past, xkey)
    row = lax.broadcasted_iota(I32, (KEY_CHUNK, r), 0)
    for c in range(n_chunks):
        rows = slice(c * KEY_CHUNK, (c + 1) * KEY_CHUNK)
        o_ref[rows, :] = jnp.where(_selected(key_scr[rows, :], row + c * KEY_CHUNK, thr, jstar), 1.0, 0.0)
    own = jnp.where(_selected(xkey, jnp.full((1, r), past, I32), thr, jstar), 1.0, 0.0)
    o_ref[past:, :] = jnp.broadcast_to(own, (o_ref.shape[0] - past, r))


def _sample_select(scores_t, k_sel):
    rows, n = scores_t.shape
    past = rows - LANES
    assert past % KEY_CHUNK == 0
    return pl.pallas_call(
        functools.partial(_sample_select_kernel, k_sel), grid=(1,),
        in_specs=[_const_spec(scores_t.shape)], out_specs=_const_spec(scores_t.shape),
        out_shape=jax.ShapeDtypeStruct(scores_t.shape, F32),
        scratch_shapes=[pltpu.VMEM((past, n), I32), pltpu.VMEM((past, n), jnp.int16)],
        compiler_params=_cparams(("arbitrary",), _mib(24)), name="sample_topk_mask",
    )(scores_t)


def _sample_attend_kernel(n_pages, pt_ref, q_ref, kn_ref, vn_ref, m_ref, *refs):
    k_refs, v_refs, o_ref = refs[:n_pages], refs[n_pages:2 * n_pages], refs[2 * n_pages]
    q8 = q_ref[0] * (A_HEAD_DIM ** -0.5)
    q8b = q8.astype(BF16)
    head_of_row = lax.broadcasted_iota(I32, (A_HEADS, PAGE_SIZE), 0)
    pages = []
    for j in range(n_pages):
        sj = jnp.zeros((A_HEADS, PAGE_SIZE), F32)
        for h in range(A_HEADS):
            sj = jnp.where(head_of_row == h, _dot(q8b, k_refs[j][0, 0, h]), sj)
        pages.append(jnp.where(m_ref[0, :, j * PAGE_SIZE:(j + 1) * PAGE_SIZE] > 0.5, sj, -jnp.inf))
    s = jnp.concatenate(pages, axis=1)
    s_own = jnp.sum(q8 * kn_ref[0], axis=1, keepdims=True)
    s_own = jnp.where(m_ref[0, :, n_pages * PAGE_SIZE:n_pages * PAGE_SIZE + 1] > 0.5, s_own, -jnp.inf)
    m = jnp.maximum(jnp.max(s, axis=1, keepdims=True), s_own)
    p = jnp.exp(s - m)
    p_own = jnp.exp(s_own - m)
    l = jnp.sum(p, axis=1, keepdims=True) + p_own
    acc = p_own * vn_ref[0]
    head_of_out = lax.broadcasted_iota(I32, (A_HEADS, A_HEAD_DIM), 0)
    for j in range(n_pages):
        pj = p[:, j * PAGE_SIZE:(j + 1) * PAGE_SIZE].astype(BF16)
        for h in range(A_HEADS):
            acc = acc + jnp.where(head_of_out == h, _dot_nt(pj, v_refs[j][0, 0, h]), 0.0)
    o_ref[0] = acc / l


def _sample_attend(page_table, q, k_new, v_new, mask, cache_k_t, cache_v_t, layer):
    n, n_pages = page_table.shape
    page = lambda j: pl.BlockSpec((1, 1, A_HEADS, A_HEAD_DIM, PAGE_SIZE), lambda i, pt: (layer, pt[i, j], 0, 0, 0))
    tok = pl.BlockSpec((1, A_HEADS, A_HEAD_DIM), lambda i, pt: (i, 0, 0))
    grid_spec = pltpu.PrefetchScalarGridSpec(
        num_scalar_prefetch=1, grid=(n,),
        in_specs=[tok, tok, tok, pl.BlockSpec((1, 1, mask.shape[-1]), lambda i, pt: (i, 0, 0))]
        + [page(j) for j in range(n_pages)] * 2,
        out_specs=tok)
    return pl.pallas_call(
        functools.partial(_sample_attend_kernel, n_pages), grid_spec=grid_spec,
        out_shape=jax.ShapeDtypeStruct((n, A_HEADS, A_HEAD_DIM), F32),
        compiler_params=_cparams(("arbitrary",), _mib(40)), name="sample_dsa_attend",
    )(page_table, q, k_new, v_new, mask, *([cache_k_t] * n_pages), *([cache_v_t] * n_pages))


def _linear_kernel(x_ref, w_ref, o_ref):
    o_ref[...] = _dot(x_ref[...], w_ref[...])


def _linear(x, w):
    n, m = x.shape[0], w.shape[1]
    return pl.pallas_call(
        _linear_kernel, grid=(1,), in_specs=[_const_spec(x.shape), _const_spec(w.shape)],
        out_specs=_const_spec((n, m)), out_shape=jax.ShapeDtypeStruct((n, m), F32),
        compiler_params=_cparams(("arbitrary",), _mib(24)), name="linear",
    )(x, w)


def _linear_res_ln_kernel(alpha, o_ref, w_ref, x_ref, g_ref, b_ref, y_ref):
    y_ref[...] = _layer_norm(alpha * x_ref[...] + _dot(o_ref[...], w_ref[...]), g_ref[...], b_ref[...])


def _linear_res_ln(alpha, o, w, x, g, b):
    args = (o, w, x, g, b)
    return pl.pallas_call(
        functools.partial(_linear_res_ln_kernel, alpha), grid=(1,), in_specs=[_const_spec(a.shape) for a in args],
        out_specs=_const_spec(x.shape), out_shape=jax.ShapeDtypeStruct(x.shape, F32),
        compiler_params=_cparams(("arbitrary",), _mib(24)), name="linear_residual_norm",
    )(*args)


def _mem_sample_kernel(bs, q_ref, mk_ref, mv_ref, o_ref):
    for i in range(bs):
        q = q_ref[i]
        s = jnp.sum(mk_ref[0, i] * q[None], axis=-1, keepdims=True) * (MEM_HEAD_DIM ** -0.5)
        m = jnp.max(s, axis=0, keepdims=True)
        p = jnp.exp(s - m)
        l = jnp.sum(p, axis=0)
        o_ref[i] = jnp.sum(p * mv_ref[0, i], axis=0) / l


def _mem_sample(q, cache_mk, cache_mv, layer, bs):
    n = q.shape[0]
    cache = pl.BlockSpec((1, bs, MEM_TOKENS, MEM_HEADS, MEM_HEAD_DIM), lambda i: (layer, i, 0, 0, 0))
    tok = pl.BlockSpec((bs, MEM_HEADS, MEM_HEAD_DIM), lambda i: (i, 0, 0))
    return pl.pallas_call(
        functools.partial(_mem_sample_kernel, bs), grid=(n // bs,),
        in_specs=[tok, cache, cache], out_specs=tok,
        out_shape=jax.ShapeDtypeStruct((n, MEM_HEADS, MEM_HEAD_DIM), F32),
        compiler_params=_cparams(("parallel",), _mib(40)), name="mem_attend_sample",
    )(q, cache_mk, cache_mv)


def _pack_w_in(w_in):
    lead = w_in.shape[:-1]
    pts = np.cumsum([A_WIDTH, A_WIDTH, A_WIDTH, IDX_HEADS * IDX_DIM, IDX_DIM, IDX_HEADS, S5_WIDTH]).tolist()
    q, k, v, iq, ik, iw, u5, zc = jnp.split(w_in, pts, axis=-1)
    pad = jnp.zeros(lead + (IKW_PAD,), w_in.dtype)
    return jnp.concatenate([q, k, v, iq, ik, iw, pad, u5, zc], axis=-1).astype(BF16)


def _rope_tables(pos):
    half = A_HEAD_DIM // 2
    inv_freq = ROPE_THETA ** (-jnp.arange(half, dtype=F32) / half)
    ang = pos.astype(F32)[:, None] * inv_freq[None, :]
    c, s = jnp.cos(ang), jnp.sin(ang)
    return jnp.tile(c, (1, LANES // half)), jnp.concatenate([-s, s] * (LANES // A_HEAD_DIM), axis=1)


def _s5_matrices(lam_re, lam_im, log_dt, b_re, b_im, c_re, c_im, d):
    dt = jnp.exp(log_dt)[:, None]
    mag = jnp.exp(dt * lam_re)
    ang = dt * lam_im
    a_re = mag * jnp.cos(ang)
    a_im = mag * jnp.sin(ang)
    den = jnp.square(lam_re) + jnp.square(lam_im)
    z_re = ((a_re - 1.0) * lam_re + a_im * lam_im) / den
    z_im = (a_im * lam_re - (a_re - 1.0) * lam_im) / den
    bb_re = z_re[..., None] * b_re - z_im[..., None] * b_im
    bb_im = z_re[..., None] * b_im + z_im[..., None] * b_re
    eye = jnp.eye(S5_GROUPS, dtype=F32)
    in_re = jnp.einsum("gph,gk->ghkp", bb_re, eye).reshape(S5_WIDTH, S5_FLAT)
    in_im = jnp.einsum("gph,gk->ghkp", bb_im, eye).reshape(S5_WIDTH, S5_FLAT)
    bblk = jnp.concatenate([in_re, in_im], axis=1).astype(BF16)
    out_re = jnp.einsum("ghp,gk->gpkh", c_re, eye).reshape(S5_FLAT, S5_WIDTH)
    out_im = jnp.einsum("ghp,gk->gpkh", -c_im, eye).reshape(S5_FLAT, S5_WIDTH)
    cblk = jnp.concatenate([out_re, out_im], axis=0).astype(BF16)
    avec = jnp.concatenate([a_re.reshape(1, S5_FLAT), a_im.reshape(1, S5_FLAT)], axis=1)
    return bblk, cblk, d.reshape(1, S5_WIDTH), avec


def _pick_tile(n, pref):
    t = min(n, pref)
    assert n % t == 0
    return t


def kernel(x_prompt, x_sample, cache_k, cache_v, cache_kidx, cache_mem_k, cache_mem_v, state_s5_re, state_s5_im,
           page_table, mem_prompt, w_in, w_a_out, s5_lambda_re, s5_lambda_im, s5_log_dt, s5_b_re, s5_b_im,
           s5_c_re, s5_c_im, s5_d, w_glu_a, w_glu_b, sg_ln_g, sg_ln_b, sg_w, sg_b, w_c_out, w_gate, b_gate,
           w_mix_out, ln1_g, ln1_b, w_mq, w_mk, w_mv, w_mo, ln2_g, ln2_b, w_ff1, w_ff2, ln3_g, ln3_b):
    depth = w_in.shape[0]
    bp, tp, _ = x_prompt.shape
    db, ds, _ = x_sample.shape
    assert ds == 1 and tp % S5_CHUNK == 0 and tp % SG_CHUNK == 0 and tp % KEY_CHUNK == 0
    n_pages = page_table.shape[1]
    past = n_pages * PAGE_SIZE
    alpha = (2 * depth) ** 0.25
    bf = lambda a: a.astype(BF16)
    row = lambda a, l: a[l][None, :]

    w_in_p = _pack_w_in(w_in)
    w_a_out_b, w_glu_a_b, w_glu_b_b, w_c_out_b = bf(w_a_out), bf(w_glu_a), bf(w_glu_b), bf(w_c_out)
    w_gate_b, w_mix_b = bf(w_gate), bf(w_mix_out)
    w_mq_b, w_mk_b, w_mv_b, w_mo_b = bf(w_mq), bf(w_mk), bf(w_mv), bf(w_mo)
    w_ff1_b, w_ff2_b = bf(w_ff1), bf(w_ff2)
    s5m = [_s5_matrices(s5_lambda_re[l], s5_lambda_im[l], s5_log_dt[l], s5_b_re[l], s5_b_im[l], s5_c_re[l],
                        s5_c_im[l], s5_d[l]) for l in range(depth)]

    n = bp * tp
    tm = _pick_tile(tp, 1024)
    cos_p, sin_p = _rope_tables(jnp.arange(tp))
    k_sel_p = min(TOPK_MAX, tp // 4)
    qb = _pick_tile(tp, 512)
    sg_bias = [jnp.repeat(sg_b[l].T, SG_GROUP_CH, axis=1) for l in range(depth)]
    mem2d = mem_prompt.reshape(bp * MEM_TOKENS, D_MODEL)
    h0 = jnp.zeros((bp, 1, 2 * S5_FLAT), F32)

    h = x_prompt.reshape(n, D_MODEL)
    kv_stack, iktp, mkp, mvp, s5p = None, [], [], [], []
    for l in range(depth):
        bblk, cblk, dvec, avec = s5m[l]
        q, iq, u5, zc, kbf, ikbf, kt, vt, vtbf, ikwt = _proj(h, w_in_p[l], cos_p, sin_p, tp, tm, False, l, depth, kv_stack)
        kv_stack = (kt, vt)
        r3 = lambda a: a.reshape(bp, tp, a.shape[-1])
        att = _dsa_prompt(r3(q), r3(iq), ikwt, r3(ikbf), r3(kbf), vtbf, k_sel_p, qb)
        u5p = u5.reshape(bp, tp // S5_CHUNK, SUBLANES, S5_SEG, S5_WIDTH).swapaxes(2, 3).reshape(bp, tp, S5_WIDTH)
        y5p, hfin = _s5_prompt(u5p, bblk, cblk, dvec, avec, h0)
        y5 = y5p.reshape(bp, tp // S5_CHUNK, S5_SEG, SUBLANES, S5_WIDTH).swapaxes(2, 3)
        sg = _sgate_prompt(zc, row(sg_ln_g, l), row(sg_ln_b, l), sg_w[l], sg_bias[l], _pick_tile(tp, 512))
        h = _merge(alpha, h, att.reshape(n, A_WIDTH), y5.reshape(n, S5_WIDTH), sg, w_a_out_b[l], w_glu_a_b[l],
                   w_glu_b_b[l], w_c_out_b[l], w_gate_b[l], row(b_gate, l), w_mix_b[l], row(ln1_g, l), row(ln1_b, l), tm)
        mk, mv = _two_linear(mem2d, w_mk_b[l], w_mv_b[l], _pick_tile(bp * MEM_TOKENS, 256))
        h = _mem_prompt(alpha, h.reshape(bp, tp, D_MODEL), mk.reshape(bp, MEM_TOKENS, MEM_WIDTH),
                        mv.reshape(bp, MEM_TOKENS, MEM_WIDTH), w_mq_b[l], w_mo_b[l], row(ln2_g, l), row(ln2_b, l),
                        tm).reshape(n, D_MODEL)
        h = _ffn(alpha, h, w_ff1_b[l], w_ff2_b[l], row(ln3_g, l), row(ln3_b, l), tm)
        iktp.append(ikwt[:, :IDX_DIM, :])
        mkp.append(mk.reshape(bp, MEM_TOKENS, MEM_HEADS, MEM_HEAD_DIM))
        mvp.append(mv.reshape(bp, MEM_TOKENS, MEM_HEADS, MEM_HEAD_DIM))
        s5p.append(hfin.reshape(bp, 2, S5_GROUPS, S5_STATE))
    y_prompt = h.reshape(bp, tp, D_MODEL)

    cache_k_t = jnp.transpose(cache_k, (0, 1, 3, 4, 2))
    cache_v_t = jnp.transpose(cache_v, (0, 1, 3, 4, 2))
    cache_kidx_t = jnp.swapaxes(cache_kidx, 2, 3)
    cos_s, sin_s = _rope_tables(jnp.full((db,), past, jnp.int32))
    k_sel_s = min(TOPK_MAX, (past + ds) // 4)
    h = x_sample.reshape(db, D_MODEL)
    kv_stack_s, ikts, s5s, sgs = None, [], [], []
    for l in range(depth):
        bblk, cblk, dvec, avec = s5m[l]
        q, iq, u5, zc, _, _, kt, vt, _, ikwt, k, v, ikw = _proj(h, w_in_p[l], cos_s, sin_s, db, db, True, l, depth, kv_stack_s)
        kv_stack_s = (kt, vt)
        head_pad = ((0, 0), (0, SUBLANES - IDX_HEADS), (0, 0))
        scores = _sample_scores(page_table, jnp.pad(iq.reshape(db, IDX_HEADS, IDX_DIM), head_pad),
                                jnp.pad(ikw[:, IDX_DIM:IDX_DIM + IDX_HEADS].reshape(db, IDX_HEADS, 1), head_pad),
                                ikw.reshape(db, 1, LANES), cache_kidx_t, l, _pick_tile(db, 4))
        mask = _sample_select(scores.reshape(db, past + LANES).T, k_sel_s).T
        hd = lambda a: a.reshape(db, A_HEADS, A_HEAD_DIM)
        att = _sample_attend(page_table, hd(q), hd(k), hd(v), mask.reshape(db, 1, past + LANES), cache_k_t, cache_v_t, l)
        h0s = jnp.concatenate([state_s5_re[l].reshape(db, S5_FLAT), state_s5_im[l].reshape(db, S5_FLAT)], axis=1)
        wd = jnp.repeat(sg_w[l][:, 0, 0], SG_GROUP_CH)[None, :]
        bd = jnp.repeat(sg_b[l][:, 0], SG_GROUP_CH)[None, :]
        y5, hnew, sg, vsg = _sample_s5_sg(u5, zc, h0s, bblk, cblk, dvec, avec, row(sg_ln_g, l), row(sg_ln_b, l), wd, bd)
        h = _merge(alpha, h, att.reshape(db, A_WIDTH), y5, sg, w_a_out_b[l], w_glu_a_b[l], w_glu_b_b[l], w_c_out_b[l],
                   w_gate_b[l], row(b_gate, l), w_mix_b[l], row(ln1_g, l), row(ln1_b, l), db)
        mq = _linear(h, w_mq_b[l])
        mo = _mem_sample(mq.reshape(db, MEM_HEADS, MEM_HEAD_DIM), cache_mem_k, cache_mem_v, l, 4)
        h = _linear_res_ln(alpha, mo.reshape(db, MEM_WIDTH), w_mo_b[l], h, row(ln2_g, l), row(ln2_b, l))
        h = _ffn(alpha, h, w_ff1_b[l], w_ff2_b[l], row(ln3_g, l), row(ln3_b, l), db)
        ikts.append(ikwt[:, :IDX_DIM, :])
        s5s.append(hnew.reshape(db, 2, S5_GROUPS, S5_STATE))
        sgs.append(vsg.reshape(db, ds, SG_WIDTH))
    y_sample = h.reshape(db, ds, D_MODEL)

    def heads_last(stack, batch, seq):
        return jnp.transpose(stack.reshape(depth, batch, A_HEADS, A_HEAD_DIM, seq), (0, 1, 4, 2, 3))

    s5p, s5s = jnp.stack(s5p), jnp.stack(s5s)
    return (y_prompt, y_sample,
            heads_last(kv_stack[0], bp, tp), heads_last(kv_stack[1], bp, tp), jnp.swapaxes(jnp.stack(iktp), 2, 3),
            jnp.stack(mkp), jnp.stack(mvp), s5p[:, :, 0], s5p[:, :, 1],
            heads_last(kv_stack_s[0], 1, db).reshape(depth, db, ds, A_HEADS, A_HEAD_DIM),
            heads_last(kv_stack_s[1], 1, db).reshape(depth, db, ds, A_HEADS, A_HEAD_DIM),
            jnp.swapaxes(jnp.stack(ikts), 2, 3).reshape(depth, db, ds, IDX_DIM),
            s5s[:, :, 0], s5s[:, :, 1], jnp.stack(sgs))
```
